```python
import jax
import jax.numpy as jnp
from jax import lax
import numpy as np

D_MODEL = 2048
BATCH = 2
SEQ = 4096
DEPTH = 4

MIX_WIDTH = 2048
CONV_WIDTH = 1024
CONV_KERNEL = 31
MOBA_HEADS = 8
MOBA_HEAD_DIM = 128
MOBA_WIDTH = MOBA_HEADS * MOBA_HEAD_DIM
MOBA_BLOCK = 256
MOBA_TOPK = 3
MOBA_QCHUNK = 32
EVEN_IN = 3 * CONV_WIDTH + 4 * MOBA_WIDTH
DSA_HEADS = 16
DSA_HEAD_DIM = 128
DSA_WIDTH = DSA_HEADS * DSA_HEAD_DIM
Q_LORA = 512
KV_LORA = 256
IDX_HEADS = 16
IDX_DIM = 64
DSA_TOPK_MAX = 256
DSA_QCHUNK = 128
ODD_IN = Q_LORA + KV_LORA + IDX_DIM + IDX_HEADS + DSA_WIDTH
N_EVEN = (DEPTH + 1) // 2
N_ODD = DEPTH // 2
EPS = 1e-6

kernel_name = 'hybrid_conformer_moba_dsa_block'


def _rmsnorm(x, g):
    xf = x.astype(jnp.float32)
    y = xf * lax.rsqrt(jnp.mean(xf * xf, axis=-1, keepdims=True) + EPS)
    return (y * g.astype(jnp.float32)).astype(x.dtype)


def _layernorm(x, g, b):
    xf = x.astype(jnp.float32)
    mu = jnp.mean(xf, axis=-1, keepdims=True)
    xc = xf - mu
    y = xc * lax.rsqrt(jnp.mean(xc * xc, axis=-1, keepdims=True) + EPS)
    return (y * g.astype(jnp.float32) + b.astype(jnp.float32)).astype(x.dtype)


def _split_cols(z, sizes):
    out = []
    off = 0
    for s in sizes:
        out.append(z[..., off:off + s])
        off += s
    return out


def _conformer_conv(a_val, a_gt, w_dw, b_dw, ln_g, ln_b, w_pw, b_pw):
    u = a_val * jax.nn.sigmoid(a_gt)
    y = lax.conv_general_dilated(
        u, w_dw[:, None, :], window_strides=(1,), padding=[(CONV_KERNEL - 1, 0)],
        dimension_numbers=('NWC', 'WIO', 'NWC'), feature_group_count=CONV_WIDTH) + b_dw
    y = jax.nn.silu(_layernorm(y, ln_g, ln_b))
    return y @ w_pw + b_pw


def _moba_attention(q, k, v):
    b, t, h, dh = q.shape
    nb = -(-t // MOBA_BLOCK)
    pad = nb * MOBA_BLOCK - t
    q = q.transpose(0, 2, 1, 3)
    k = jnp.pad(k.transpose(0, 2, 1, 3), ((0, 0), (0, 0), (0, pad), (0, 0)))
    v = jnp.pad(v.transpose(0, 2, 1, 3), ((0, 0), (0, 0), (0, pad), (0, 0)))
    kb = k.reshape(b, h, nb, MOBA_BLOCK, dh)
    vb = v.reshape(b, h, nb, MOBA_BLOCK, dh)
    n_sel = min(MOBA_TOPK, nb - 1)
    scale = dh ** -0.5
    sel = None
    if n_sel > 0:
        k_mean = jnp.mean(kb, axis=3)
        gate = jnp.einsum('bhtd,bhnd->bhtn', q, k_mean)
        cur = jnp.arange(t) // MOBA_BLOCK
        fully_past = jnp.arange(nb)[None, :] < cur[:, None]
        gate = jnp.where(fully_past, gate, -jnp.inf)
        sel = lax.top_k(gate, n_sel)[1]
    bi = jnp.arange(b)[:, None, None, None]
    hi = jnp.arange(h)[None, :, None, None]

    def chunk(c):
        t0 = c * MOBA_QCHUNK
        blk = t0 // MOBA_BLOCK
        qc = lax.dynamic_slice_in_dim(q, t0, MOBA_QCHUNK, axis=2)
        k_own = lax.dynamic_index_in_dim(kb, blk, axis=2, keepdims=False)
        v_own = lax.dynamic_index_in_dim(vb, blk, axis=2, keepdims=False)
        qpos = t0 + jnp.arange(MOBA_QCHUNK)
        kpos = blk * MOBA_BLOCK + jnp.arange(MOBA_BLOCK)
        s_own = jnp.einsum('bhqd,bhsd->bhqs', qc, k_own) * scale
        s_own = jnp.where(kpos[None, :] <= qpos[:, None], s_own, -jnp.inf)
        if n_sel == 0:
            p = jax.nn.softmax(s_own.astype(jnp.float32), axis=-1).astype(v.dtype)
            return jnp.einsum('bhqs,bhsd->bhqd', p, v_own)
        sel_c = lax.dynamic_slice_in_dim(sel, t0, MOBA_QCHUNK, axis=2)
        k_sel = kb[bi, hi, sel_c]
        v_sel = vb[bi, hi, sel_c]
        s_past = jnp.einsum('bhqd,bhqksd->bhqks', qc, k_sel) * scale
        s_past = jnp.where((sel_c < blk)[..., None], s_past, -jnp.inf)
        logits = jnp.concatenate(
            [s_past.reshape(b, h, MOBA_QCHUNK, n_sel * MOBA_BLOCK), s_own], axis=-1)
        p = jax.nn.softmax(logits.astype(jnp.float32), axis=-1).astype(v.dtype)
        p_past = p[..., :n_sel * MOBA_BLOCK].reshape(b, h, MOBA_QCHUNK, n_sel, MOBA_BLOCK)
        p_own = p[..., n_sel * MOBA_BLOCK:]
        return (jnp.einsum('bhqks,bhqksd->bhqd', p_past, v_sel)
                + jnp.einsum('bhqs,bhsd->bhqd', p_own, v_own))

    out = lax.map(chunk, jnp.arange(t // MOBA_QCHUNK))
    return out.transpose(1, 0, 3, 2, 4).reshape(b, t, h * dh)


def _dsa_attention(q_lat, ckv, iq, ik, iw):
    b, t, h, c = q_lat.shape
    k_top = min(DSA_TOPK_MAX, t // 4)
    scale = DSA_HEAD_DIM ** -0.5
    bi = jnp.arange(b)[:, None, None]
    kpos = jnp.arange(t)

    def chunk(ci):
        t0 = ci * DSA_QCHUNK
        qpos = t0 + jnp.arange(DSA_QCHUNK)
        iq_c = lax.dynamic_slice_in_dim(iq, t0, DSA_QCHUNK, axis=1)
        iw_c = lax.dynamic_slice_in_dim(iw, t0, DSA_QCHUNK, axis=1)
        ql_c = lax.dynamic_slice_in_dim(q_lat, t0, DSA_QCHUNK, axis=1)
        idx = jnp.einsum('bqhd,bsd->bqhs', iq_c, ik)
        score = jnp.einsum('bqhs,bqh->bqs', jax.nn.relu(idx), iw_c)
        score = jnp.where(kpos[None, :] <= qpos[:, None], score, -jnp.inf)
        sel = lax.top_k(score, k_top)[1]
        kv = ckv[bi, sel]
        s = jnp.einsum('bqhc,bqkc->bqhk', ql_c, kv) * scale
        s = jnp.where((sel <= qpos[None, :, None])[:, :, None, :], s, -jnp.inf)
        p = jax.nn.softmax(s.astype(jnp.float32), axis=-1).astype(kv.dtype)
        return jnp.einsum('bqhk,bqkc->bqhc', p, kv)

    out = lax.map(chunk, jnp.arange(t // DSA_QCHUNK))
    return out.transpose(1, 0, 2, 3, 4).reshape(b, t, h, c)


def _even_layer(x, norm_g, w_in, b_glu, w_dw, b_dw, ln_g, ln_b, w_pw, b_pw, w_out):
    b, t, _ = x.shape
    z = _rmsnorm(x, norm_g) @ w_in
    a_val, a_gt, a_gate, q, k, v, b_gate = _split_cols(
        z, [CONV_WIDTH, CONV_WIDTH, CONV_WIDTH, MOBA_WIDTH, MOBA_WIDTH, MOBA_WIDTH, MOBA_WIDTH])
    ya = _conformer_conv(a_val + b_glu[:CONV_WIDTH], a_gt + b_glu[CONV_WIDTH:],
                         w_dw, b_dw, ln_g, ln_b, w_pw, b_pw) * jax.nn.silu(a_gate)
    shp = (b, t, MOBA_HEADS, MOBA_HEAD_DIM)
    yb = _moba_attention(q.reshape(shp), k.reshape(shp), v.reshape(shp)) * jax.nn.silu(b_gate)
    return jnp.concatenate([ya, yb], axis=-1) @ w_out


def _odd_layer(x, norm_g, w_in, q_norm, w_qb, kv_norm, w_uk, w_uv, w_iq, ik_g, ik_b, w_out):
    b, t, _ = x.shape
    z = _rmsnorm(x, norm_g) @ w_in
    cq, ckv, ik, iw, gate = _split_cols(z, [Q_LORA, KV_LORA, IDX_DIM, IDX_HEADS, DSA_WIDTH])
    cq = _rmsnorm(cq, q_norm)
    q = (cq @ w_qb).reshape(b, t, DSA_HEADS, DSA_HEAD_DIM)
    q_lat = jnp.einsum('bthd,hdc->bthc', q, w_uk)
    ckv = _rmsnorm(ckv, kv_norm)
    iq = (cq @ w_iq).reshape(b, t, IDX_HEADS, IDX_DIM)
    ik = _layernorm(ik, ik_g, ik_b)
    iw = iw * (IDX_HEADS ** -0.5 * IDX_DIM ** -0.5)
    o_lat = _dsa_attention(q_lat, ckv, iq, ik, iw)
    o = jnp.einsum('bthc,hcd->bthd', o_lat, w_uv).reshape(b, t, DSA_WIDTH)
    return (o * jax.nn.silu(gate)) @ w_out


def setup_inputs(seed: int = 0) -> dict:
    key = jax.random.key(seed)
    ks = jax.random.split(key, 24)
    f32 = jnp.float32
    out_mult = (2.0 * DEPTH) ** -0.5

    def w(k, shape, fan_in, mult=1.0):
        return jax.random.normal(k, shape, f32) * (mult * fan_in ** -0.5)

    def gain(k, shape):
        return 1.0 + 0.02 * jax.random.normal(k, shape, f32)

    def bias(k, shape):
        return 0.02 * jax.random.normal(k, shape, f32)

    return {
        'x': jax.random.normal(ks[0], (BATCH, SEQ, D_MODEL), f32),
        'even_norm': gain(ks[1], (N_EVEN, D_MODEL)),
        'even_w_in': w(ks[2], (N_EVEN, D_MODEL, EVEN_IN), D_MODEL),
        'even_b_glu': bias(ks[3], (N_EVEN, 2 * CONV_WIDTH)),
        'even_w_dw': w(ks[4], (N_EVEN, CONV_KERNEL, CONV_WIDTH), CONV_KERNEL),
        'even_b_dw': bias(ks[5], (N_EVEN, CONV_WIDTH)),
        'even_conv_ln_g': gain(ks[6], (N_EVEN, CONV_WIDTH)),
        'even_conv_ln_b': bias(ks[7], (N_EVEN, CONV_WIDTH)),
        'even_w_pw': w(ks[8], (N_EVEN, CONV_WIDTH, CONV_WIDTH), CONV_WIDTH),
        'even_b_pw': bias(ks[9], (N_EVEN, CONV_WIDTH)),
        'even_w_out': w(ks[10], (N_EVEN, MIX_WIDTH, D_MODEL), MIX_WIDTH, out_mult),
        'odd_norm': gain(ks[11], (N_ODD, D_MODEL)),
        'odd_w_in': w(ks[12], (N_ODD, D_MODEL, ODD_IN), D_MODEL),
        'odd_q_norm': gain(ks[13], (N_ODD, Q_LORA)),
        'odd_w_qb': w(ks[14], (N_ODD, Q_LORA, DSA_WIDTH), Q_LORA),
        'odd_kv_norm': gain(ks[15], (N_ODD, KV_LORA)),
        'odd_w_uk': w(ks[16], (N_ODD, DSA_HEADS, DSA_HEAD_DIM, KV_LORA), DSA_HEAD_DIM),
        'odd_w_uv': w(ks[17], (N_ODD, DSA_HEADS, KV_LORA, DSA_HEAD_DIM), KV_LORA),
        'odd_w_iq': w(ks[18], (N_ODD, Q_LORA, IDX_HEADS * IDX_DIM), Q_LORA),
        'odd_ik_ln_g': gain(ks[19], (N_ODD, IDX_DIM)),
        'odd_ik_ln_b': bias(ks[20], (N_ODD, IDX_DIM)),
        'odd_w_out': w(ks[21], (N_ODD, DSA_WIDTH, D_MODEL), DSA_WIDTH, out_mult),
        'final_norm': gain(ks[22], (D_MODEL,)),
    }


def reference(x, even_norm, even_w_in, even_b_glu, even_w_dw, even_b_dw, even_conv_ln_g,
              even_conv_ln_b, even_w_pw, even_b_pw, even_w_out, odd_norm, odd_w_in, odd_q_norm,
              odd_w_qb, odd_kv_norm, odd_w_uk, odd_w_uv, odd_w_iq, odd_ik_ln_g, odd_ik_ln_b,
              odd_w_out, final_norm):
    for layer in range(DEPTH):
        i = layer // 2
        if layer % 2 == 0:
            x = x + _even_layer(x, even_norm[i], even_w_in[i], even_b_glu[i], even_w_dw[i],
                                even_b_dw[i], even_conv_ln_g[i], even_conv_ln_b[i],
                                even_w_pw[i], even_b_pw[i], even_w_out[i])
        else:
            x = x + _odd_layer(x, odd_norm[i], odd_w_in[i], odd_q_norm[i], odd_w_qb[i],
                               odd_kv_norm[i], odd_w_uk[i], odd_w_uv[i], odd_w_iq[i],
                               odd_ik_ln_g[i], odd_ik_ln_b[i], odd_w_out[i])
    return _rmsnorm(x, final_norm)
```

```python
import functools

import jax
import jax.numpy as jnp
from jax import lax
from jax.experimental import pallas as pl
from jax.experimental.pallas import tpu as pltpu

BF16 = jnp.bfloat16
F32 = jnp.float32
I32 = jnp.int32

EPS = 1e-6
LANES = 128
VMEM_LIMIT = 56 * 1024 * 1024

CONV_KERNEL = 31
CONV_HALO = 32
MOBA_HEAD_DIM = 128
MOBA_BLOCK = 256
MOBA_TOPK = 3
DSA_HEADS = 16
DSA_HEAD_DIM = 128
Q_LORA = 512
KV_LORA = 256
IDX_HEADS = 16
IDX_DIM = 64
DSA_TOPK_MAX = 256
DSA_TQ = 128
DSA_TK = 256
NEG = -1e30
INT_MIN = -2 ** 31

NT_DIMS = (((1,), (1,)), ((), ()))


def _params(*sem):
    return pltpu.CompilerParams(dimension_semantics=sem, vmem_limit_bytes=VMEM_LIMIT)


def _rms(x, g):
    return x * lax.rsqrt(jnp.mean(x * x, axis=-1, keepdims=True) + EPS) * g


def _silu(x):
    return x * jax.nn.sigmoid(x)


def _norm_matmul_kernel(x_ref, g_ref, w_ref, o_ref, xn_ref):
    @pl.when(pl.program_id(1) == 0)
    def _():
        xn_ref[...] = _rms(x_ref[...], g_ref[...]).astype(BF16)

    o_ref[...] = jnp.dot(xn_ref[...], w_ref[...], preferred_element_type=F32).astype(o_ref.dtype)


def _norm_matmul(x2, g, w, tm, tn):
    n, d = x2.shape
    m = w.shape[1]
    return pl.pallas_call(
        _norm_matmul_kernel,
        grid=(n // tm, m // tn),
        in_specs=[
            pl.BlockSpec((tm, d), lambda i, j: (i, 0)),
            pl.BlockSpec((1, d), lambda i, j: (0, 0)),
            pl.BlockSpec((d, tn), lambda i, j: (0, j)),
        ],
        out_specs=pl.BlockSpec((tm, tn), lambda i, j: (i, j)),
        out_shape=jax.ShapeDtypeStruct((n, m), BF16),
        scratch_shapes=[pltpu.VMEM((tm, d), BF16)],
        compiler_params=_params("parallel", "arbitrary"),
        name="norm_matmul",
    )(x2, g.reshape(1, d), w)


def _proj_residual_kernel(n_in, final, *refs):
    x_ref = refs[0]
    a_refs = refs[1:1 + n_in]
    w_refs = refs[1 + n_in:1 + 2 * n_in]
    g_ref = refs[1 + 2 * n_in]
    o_ref = refs[2 + 2 * n_in]
    y = x_ref[...]
    for a_ref, w_ref in zip(a_refs, w_refs):
        y = y + jnp.dot(a_ref[...], w_ref[...], preferred_element_type=F32)
    if final:
        y = _rms(y, g_ref[...])
    o_ref[...] = y


def _proj_residual(x2, acts, ws, final_g, final, tm):
    n, d = x2.shape
    n_in = len(acts)
    in_specs = [pl.BlockSpec((tm, d), lambda i: (i, 0))]
    in_specs += [pl.BlockSpec((tm, a.shape[1]), lambda i: (i, 0)) for a in acts]
    in_specs += [pl.BlockSpec(w.shape, lambda i: (0, 0)) for w in ws]
    in_specs += [pl.BlockSpec((1, d), lambda i: (0, 0))]
    return pl.pallas_call(
        functools.partial(_proj_residual_kernel, n_in, final),
        grid=(n // tm,),
        in_specs=in_specs,
        out_specs=pl.BlockSpec((tm, d), lambda i: (i, 0)),
        out_shape=jax.ShapeDtypeStruct((n, d), F32),
        compiler_params=_params("parallel"),
        name="proj_residual",
    )(x2, *acts, *ws, final_g.reshape(1, d))


def _conv_kernel(tt, av_ref, ag_ref, gate_ref, hv_ref, hg_ref, bv_ref, bg_ref, wdw_ref, bdw_ref,
                 lng_ref, lnb_ref, wpw_ref, bpw_ref, o_ref, ubuf):
    i = pl.program_id(1)
    bv = bv_ref[...]
    bg = bg_ref[...]

    def glu(v, g):
        return (v.astype(F32) + bv) * jax.nn.sigmoid(g.astype(F32) + bg)

    ubuf[0:CONV_HALO, :] = jnp.where(i > 0, glu(hv_ref[...], hg_ref[...]), 0.0)
    ubuf[CONV_HALO:, :] = glu(av_ref[...], ag_ref[...])
    shift = CONV_HALO - (CONV_KERNEL - 1)
    acc = wdw_ref[0:1, :] * ubuf[shift:shift + tt, :]
    for j in range(1, CONV_KERNEL):
        acc = acc + wdw_ref[j:j + 1, :] * ubuf[shift + j:shift + j + tt, :]
    y = acc + bdw_ref[...]
    mu = jnp.mean(y, axis=-1, keepdims=True)
    yc = y - mu
    var = jnp.mean(yc * yc, axis=-1, keepdims=True)
    y = _silu(yc * lax.rsqrt(var + EPS) * lng_ref[...] + lnb_ref[...])
    y = jnp.dot(y.astype(BF16), wpw_ref[...], preferred_element_type=F32) + bpw_ref[...]
    o_ref[...] = (y * _silu(gate_ref[...].astype(F32))).astype(o_ref.dtype)


def _conv_branch(z3, b_glu, w_dw, b_dw, ln_g, ln_b, w_pw, b_pw, tt):
    b, t, _ = z3.shape
    c = w_pw.shape[0]
    hb = tt // CONV_HALO
    row = lambda v: v.reshape(1, c).astype(F32)
    w_dw_p = jnp.pad(w_dw.astype(F32), ((0, 32 - CONV_KERNEL), (0, 0)))
    const = lambda shape: pl.BlockSpec(shape, lambda bi, i: (0, 0))
    return pl.pallas_call(
        functools.partial(_conv_kernel, tt),
        grid=(b, t // tt),
        in_specs=[
            pl.BlockSpec((None, tt, c), lambda bi, i: (bi, i, 0)),
            pl.BlockSpec((None, tt, c), lambda bi, i: (bi, i, 1)),
            pl.BlockSpec((None, tt, c), lambda bi, i: (bi, i, 2)),
            pl.BlockSpec((None, CONV_HALO, c), lambda bi, i: (bi, jnp.maximum(i * hb - 1, 0), 0)),
            pl.BlockSpec((None, CONV_HALO, c), lambda bi, i: (bi, jnp.maximum(i * hb - 1, 0), 1)),
            const((1, c)), const((1, c)), const((32, c)), const((1, c)), const((1, c)), const((1, c)),
            const((c, c)), const((1, c)),
        ],
        out_specs=pl.BlockSpec((None, tt, c), lambda bi, i: (bi, i, 0)),
        out_shape=jax.ShapeDtypeStruct((b, t, c), BF16),
        scratch_shapes=[pltpu.VMEM((tt + CONV_HALO, c), F32)],
        compiler_params=_params("parallel", "parallel"),
        name="conv_branch",
    )(z3, z3, z3, z3, z3, row(b_glu[:c]), row(b_glu[c:]), w_dw_p, row(b_dw), row(ln_g), row(ln_b),
      w_pw, row(b_pw))


def _moba_kernel(nb, q_ref, k_ref, v_ref, g_ref, o_ref, kmean_ref):
    i = pl.program_id(2)
    blk = MOBA_BLOCK
    scale = MOBA_HEAD_DIM ** -0.5

    @pl.when(i == 0)
    def _():
        kmean_ref[...] = jnp.zeros_like(kmean_ref)
        for n in range(nb):
            kb = k_ref[n * blk:(n + 1) * blk, :].astype(F32)
            kmean_ref[n:n + 1, :] = jnp.mean(kb, axis=0, keepdims=True)

    q = q_ref[...]
    gate = lax.dot_general(q, kmean_ref[...].astype(BF16), NT_DIMS, preferred_element_type=F32)
    lane = lax.broadcasted_iota(I32, (blk, LANES), 1)
    past = lane < i
    gate = jnp.where(past, gate, -jnp.inf)
    rank = jnp.zeros((blk, LANES), I32)
    for jp in range(nb - 1):
        col = gate[:, jp:jp + 1]
        beats = jnp.where(col > gate, 1, jnp.where(col == gate, jnp.where(jp < lane, 1, 0), 0))
        rank = rank + beats
    sel = jnp.where(past, jnp.where(rank < MOBA_TOPK, 1.0, 0.0), 0.0)

    def step(j, carry, mask):
        m, l, acc = carry
        k0 = pl.multiple_of(j * blk, blk)
        s = lax.dot_general(q, k_ref[pl.ds(k0, blk), :], NT_DIMS, preferred_element_type=F32) * scale
        s = jnp.where(mask, s, NEG)
        m_new = jnp.maximum(m, jnp.max(s, axis=1, keepdims=True))
        alpha = jnp.exp(m - m_new)
        p = jnp.exp(s - m_new)
        l = alpha * l + jnp.sum(p, axis=1, keepdims=True)
        acc = alpha * acc + jnp.dot(p.astype(BF16), v_ref[pl.ds(k0, blk), :], preferred_element_type=F32)
        return m_new, l, acc

    def past_step(j, carry):
        selj = jnp.sum(jnp.where(lane == j, sel, 0.0), axis=1, keepdims=True) > 0.5
        return step(j, carry, selj)

    init = (jnp.full((blk, 1), NEG, F32), jnp.zeros((blk, 1), F32), jnp.zeros((blk, MOBA_HEAD_DIM), F32))
    carry = lax.fori_loop(0, i, past_step, init)
    r = lax.broadcasted_iota(I32, (blk, blk), 0)
    c = lax.broadcasted_iota(I32, (blk, blk), 1)
    m, l, acc = step(i, carry, c <= r)
    o_ref[...] = (acc / l * _silu(g_ref[...].astype(F32))).astype(o_ref.dtype)


def _moba_branch(z3, heads, col0):
    b, t, _ = z3.shape
    nb = t // MOBA_BLOCK
    c0 = col0 // MOBA_HEAD_DIM
    return pl.pallas_call(
        functools.partial(_moba_kernel, nb),
        grid=(b, heads, nb),
        in_specs=[
            pl.BlockSpec((None, MOBA_BLOCK, MOBA_HEAD_DIM), lambda bi, h, i: (bi, i, c0 + h)),
            pl.BlockSpec((None, t, MOBA_HEAD_DIM), lambda bi, h, i: (bi, 0, c0 + heads + h)),
            pl.BlockSpec((None, t, MOBA_HEAD_DIM), lambda bi, h, i: (bi, 0, c0 + 2 * heads + h)),
            pl.BlockSpec((None, MOBA_BLOCK, MOBA_HEAD_DIM), lambda bi, h, i: (bi, i, c0 + 3 * heads + h)),
        ],
        out_specs=pl.BlockSpec((None, MOBA_BLOCK, MOBA_HEAD_DIM), lambda bi, h, i: (bi, i, h)),
        out_shape=jax.ShapeDtypeStruct((b, t, heads * MOBA_HEAD_DIM), BF16),
        scratch_shapes=[pltpu.VMEM((LANES, MOBA_HEAD_DIM), F32)],
        compiler_params=_params("parallel", "parallel", "arbitrary"),
        name="moba_branch",
    )(z3, z3, z3, z3)


def _dsa_prep_kernel(x_ref, g_ref, ws_ref, qn_ref, wqb_ref, kvn_ref, wuk_ref, wiq_ref, ikg_ref, ikb_ref,
                     qlat_ref, iq_ref, ckv_ref, ik2_ref, iw_ref):
    scale = DSA_HEAD_DIM ** -0.5
    xn = _rms(x_ref[...], g_ref[...]).astype(BF16)
    zs = jnp.dot(xn, ws_ref[...], preferred_element_type=F32)
    o_kv = Q_LORA
    o_ik = Q_LORA + KV_LORA
    o_iw = o_ik + 2 * LANES
    cqn = _rms(zs[:, :Q_LORA], qn_ref[...]).astype(BF16)
    q = jnp.dot(cqn, wqb_ref[...], preferred_element_type=F32).astype(BF16)
    for h in range(DSA_HEADS):
        ql = jnp.dot(q[:, h * DSA_HEAD_DIM:(h + 1) * DSA_HEAD_DIM], wuk_ref[h], preferred_element_type=F32)
        qlat_ref[h] = (ql * scale).astype(qlat_ref.dtype)
    iq = jnp.dot(cqn, wiq_ref[...], preferred_element_type=F32)
    for p in range(IDX_HEADS // 2):
        iq_ref[p] = iq[:, p * LANES:(p + 1) * LANES].astype(iq_ref.dtype)
    ckv_ref[...] = _rms(zs[:, o_kv:o_kv + KV_LORA], kvn_ref[...]).astype(ckv_ref.dtype)
    lane = lax.broadcasted_iota(I32, (1, LANES), 1)
    for half in range(2):
        z = zs[:, o_ik + half * LANES:o_ik + (half + 1) * LANES]
        valid = (lane < IDX_DIM) if half == 0 else (lane >= IDX_DIM)
        mu = jnp.sum(z, axis=-1, keepdims=True) * (1.0 / IDX_DIM)
        zc = jnp.where(valid, z - mu, 0.0)
        var = jnp.sum(zc * zc, axis=-1, keepdims=True) * (1.0 / IDX_DIM)
        y = zc * lax.rsqrt(var + EPS) * ikg_ref[:, half * LANES:(half + 1) * LANES] \
            + ikb_ref[:, half * LANES:(half + 1) * LANES]
        ik2_ref[:, half * LANES:(half + 1) * LANES] = y.astype(ik2_ref.dtype)
    iw_ref[...] = zs[:, o_iw:o_iw + LANES] * (IDX_HEADS ** -0.5 * IDX_DIM ** -0.5)


def _dsa_prep(x3, norm_g, w_small, q_norm, w_qb, kv_norm, w_uk, w_iq, ik_g2, ik_b2, tm):
    b, t, d = x3.shape
    const = lambda shape: pl.BlockSpec(shape, lambda bi, i: (0,) * len(shape))
    return pl.pallas_call(
        _dsa_prep_kernel,
        grid=(b, t // tm),
        in_specs=[
            pl.BlockSpec((None, tm, d), lambda bi, i: (bi, i, 0)),
            const((1, d)), const(w_small.shape), const((1, Q_LORA)), const(w_qb.shape),
            const((1, KV_LORA)), const(w_uk.shape), const(w_iq.shape), const((1, 2 * LANES)),
            const((1, 2 * LANES)),
        ],
        out_specs=[
            pl.BlockSpec((None, DSA_HEADS, tm, KV_LORA), lambda bi, i: (bi, 0, i, 0)),
            pl.BlockSpec((None, IDX_HEADS // 2, tm, LANES), lambda bi, i: (bi, 0, i, 0)),
            pl.BlockSpec((None, tm, KV_LORA), lambda bi, i: (bi, i, 0)),
            pl.BlockSpec((None, tm, 2 * LANES), lambda bi, i: (bi, i, 0)),
            pl.BlockSpec((None, tm, LANES), lambda bi, i: (bi, i, 0)),
        ],
        out_shape=[
            jax.ShapeDtypeStruct((b, DSA_HEADS, t, KV_LORA), BF16),
            jax.ShapeDtypeStruct((b, IDX_HEADS // 2, t, LANES), BF16),
            jax.ShapeDtypeStruct((b, t, KV_LORA), BF16),
            jax.ShapeDtypeStruct((b, t, 2 * LANES), BF16),
            jax.ShapeDtypeStruct((b, t, LANES), F32),
        ],
        compiler_params=_params("parallel", "parallel"),
        name="dsa_prep",
    )(x3, norm_g.reshape(1, d), w_small, q_norm.reshape(1, Q_LORA), w_qb, kv_norm.reshape(1, KV_LORA),
      w_uk, w_iq, ik_g2, ik_b2)


def _dsa_attn_kernel(k_top, iq_ref, iw_ref, ik2_ref, qlat_ref, ckv_ref, gate_ref, wuv_ref, o_ref,
                     keys_ref, m_ref, l_ref, acc_ref):
    tq, tk = DSA_TQ, DSA_TK
    i = pl.program_id(1)
    nkb = ((i + 1) * tq + tk - 1) // tk
    qpos = i * tq + lax.broadcasted_iota(I32, (tq, tk), 0)
    kcol = lax.broadcasted_iota(I32, (tq, tk), 1)

    iq_all = iq_ref[...].reshape(IDX_HEADS // 2 * tq, LANES)
    iw = iw_ref[...]

    def score_blk(j, _):
        k0 = pl.multiple_of(j * tk, tk)
        se = lax.dot_general(iq_all, ik2_ref[pl.ds(k0, tk), 0:LANES], NT_DIMS, preferred_element_type=F32)
        so = lax.dot_general(iq_all, ik2_ref[pl.ds(k0, tk), LANES:2 * LANES], NT_DIMS,
                             preferred_element_type=F32)
        sc = jnp.zeros((tq, tk), F32)
        for p in range(IDX_HEADS // 2):
            sc = sc + jnp.maximum(se[p * tq:(p + 1) * tq], 0.0) * iw[:, 2 * p:2 * p + 1]
            sc = sc + jnp.maximum(so[p * tq:(p + 1) * tq], 0.0) * iw[:, 2 * p + 1:2 * p + 2]
        sc = jnp.where(k0 + kcol <= qpos, sc, -jnp.inf)
        bits = pltpu.bitcast(sc, I32)
        keys_ref[:, pl.ds(k0, tk)] = bits ^ ((bits >> 31) & 0x7FFFFFFF)
        return 0

    lax.fori_loop(0, nkb, score_blk, 0)

    def bit_body(b, ans):
        cand = ans + lax.shift_left(jnp.int32(1), 31 - b)
        candb = jnp.broadcast_to(cand, (tq, tk))

        def col_body(j, cnt):
            kb = keys_ref[:, pl.ds(pl.multiple_of(j * tk, tk), tk)]
            ge = jnp.where(kb >= candb, 1, 0)
            return cnt + ge[:, :LANES] + ge[:, LANES:]

        cnt = lax.fori_loop(0, nkb, col_body, jnp.zeros((tq, LANES), I32))
        total = jnp.sum(cnt, axis=1, keepdims=True)
        return jnp.where(total >= k_top, cand, ans)

    thr = lax.fori_loop(0, 32, bit_body, jnp.full((tq, 1), INT_MIN, I32))

    q_all = qlat_ref[...].reshape(DSA_HEADS * tq, KV_LORA)
    m_ref[...] = jnp.full_like(m_ref, NEG)
    l_ref[...] = jnp.zeros_like(l_ref)
    acc_ref[...] = jnp.zeros_like(acc_ref)

    def attn_blk(j, _):
        k0 = pl.multiple_of(j * tk, tk)
        kv = ckv_ref[pl.ds(k0, tk), :]
        s_all = lax.dot_general(q_all, kv, NT_DIMS, preferred_element_type=F32)
        kb = keys_ref[:, pl.ds(k0, tk)]
        bias = jnp.where(k0 + kcol <= qpos, jnp.where(kb >= thr, 0.0, NEG), NEG)
        for h in range(DSA_HEADS):
            rows = slice(h * tq, (h + 1) * tq)
            s = s_all[rows] + bias
            m_old = m_ref[rows]
            m_new = jnp.maximum(m_old, jnp.max(s, axis=1, keepdims=True))
            alpha = jnp.exp(m_old - m_new)
            p = jnp.exp(s - m_new)
            l_ref[rows] = alpha * l_ref[rows] + jnp.sum(p, axis=1, keepdims=True)
            acc_ref[rows] = alpha * acc_ref[rows] + jnp.dot(p.astype(BF16), kv, preferred_element_type=F32)
            m_ref[rows] = m_new
        return 0

    lax.fori_loop(0, nkb, attn_blk, 0)

    for h in range(DSA_HEADS):
        rows = slice(h * tq, (h + 1) * tq)
        o_lat = (acc_ref[rows] / l_ref[rows]).astype(BF16)
        o = jnp.dot(o_lat, wuv_ref[h], preferred_element_type=F32)
        cols = slice(h * DSA_HEAD_DIM, (h + 1) * DSA_HEAD_DIM)
        o_ref[:, cols] = (o * _silu(gate_ref[:, cols].astype(F32))).astype(o_ref.dtype)


def _dsa_attn(iq, iw, ik2, qlat, ckv, gate3, w_uv):
    b, t, _ = ckv.shape
    k_top = min(DSA_TOPK_MAX, t // 4)
    tq = DSA_TQ
    width = DSA_HEADS * DSA_HEAD_DIM
    return pl.pallas_call(
        functools.partial(_dsa_attn_kernel, k_top),
        grid=(b, t // tq),
        in_specs=[
            pl.BlockSpec((None, IDX_HEADS // 2, tq, LANES), lambda bi, i: (bi, 0, i, 0)),
            pl.BlockSpec((None, tq, LANES), lambda bi, i: (bi, i, 0)),
            pl.BlockSpec((None, t, 2 * LANES), lambda bi, i: (bi, 0, 0)),
            pl.BlockSpec((None, DSA_HEADS, tq, KV_LORA), lambda bi, i: (bi, 0, i, 0)),
            pl.BlockSpec((None, t, KV_LORA), lambda bi, i: (bi, 0, 0)),
            pl.BlockSpec((None, tq, width), lambda bi, i: (bi, i, 0)),
            pl.BlockSpec(w_uv.shape, lambda bi, i: (0, 0, 0)),
        ],
        out_specs=pl.BlockSpec((None, tq, width), lambda bi, i: (bi, i, 0)),
        out_shape=jax.ShapeDtypeStruct((b, t, width), BF16),
        scratch_shapes=[
            pltpu.VMEM((tq, t), I32),
            pltpu.VMEM((DSA_HEADS * tq, 1), F32),
            pltpu.VMEM((DSA_HEADS * tq, 1), F32),
            pltpu.VMEM((DSA_HEADS * tq, KV_LORA), F32),
        ],
        compiler_params=_params("parallel", "parallel"),
        name="dsa_attn",
    )(iq, iw, ik2, qlat, ckv, gate3, w_uv)


def _even_layer(x2, b, t, norm_g, w_in, b_glu, w_dw, b_dw, ln_g, ln_b, w_pw, b_pw, w_out, final_g, final):
    c = w_pw.shape[0]
    heads = (w_in.shape[1] - 3 * c) // (4 * MOBA_HEAD_DIM)
    z = _norm_matmul(x2, norm_g, w_in.astype(BF16), tm=1024, tn=512)
    z3 = z.reshape(b, t, z.shape[1])
    ya = _conv_branch(z3, b_glu, w_dw, b_dw, ln_g, ln_b, w_pw.astype(BF16), b_pw, tt=256)
    yb = _moba_branch(z3, heads, 3 * c)
    w_out = w_out.astype(BF16)
    return _proj_residual(x2, [ya.reshape(b * t, c), yb.reshape(b * t, -1)], [w_out[:c], w_out[c:]],
                          final_g, final, tm=512)


def _odd_layer(x2, b, t, norm_g, w_in, q_norm, w_qb, kv_norm, w_uk, w_uv, w_iq, ik_g, ik_b, w_out,
               final_g, final):
    d = x2.shape[1]
    o_ik = Q_LORA + KV_LORA
    o_iw = o_ik + IDX_DIM
    o_gate = o_iw + IDX_HEADS
    zeros = lambda n: jnp.zeros((d, n), w_in.dtype)
    w_ik = w_in[:, o_ik:o_iw]
    w_small = jnp.concatenate(
        [w_in[:, :o_ik], w_ik, zeros(LANES - IDX_DIM), zeros(LANES - IDX_DIM), w_ik,
         w_in[:, o_iw:o_gate], zeros(LANES - IDX_HEADS)], axis=1).astype(BF16)
    pad = jnp.zeros((IDX_DIM,), F32)
    ik_g2 = jnp.concatenate([ik_g, pad, pad, ik_g]).reshape(1, 2 * LANES)
    ik_b2 = jnp.concatenate([ik_b, pad, pad, ik_b]).reshape(1, 2 * LANES)
    x3 = x2.reshape(b, t, d)
    qlat, iq, ckv, ik2, iw = _dsa_prep(x3, norm_g, w_small, q_norm, w_qb.astype(BF16), kv_norm,
                                       w_uk.astype(BF16), w_iq.astype(BF16), ik_g2, ik_b2, tm=256)
    gate = _norm_matmul(x2, norm_g, w_in[:, o_gate:].astype(BF16), tm=1024, tn=512)
    og = _dsa_attn(iq, iw, ik2, qlat, ckv, gate.reshape(b, t, -1), w_uv.astype(BF16))
    return _proj_residual(x2, [og.reshape(b * t, -1)], [w_out.astype(BF16)], final_g, final, tm=512)


def kernel(x, even_norm, even_w_in, even_b_glu, even_w_dw, even_b_dw, even_conv_ln_g, even_conv_ln_b,
           even_w_pw, even_b_pw, even_w_out, odd_norm, odd_w_in, odd_q_norm, odd_w_qb, odd_kv_norm,
           odd_w_uk, odd_w_uv, odd_w_iq, odd_ik_ln_g, odd_ik_ln_b, odd_w_out, final_norm):
    b, t, d = x.shape
    depth = even_norm.shape[0] + odd_norm.shape[0]
    x2 = x.reshape(b * t, d)
    for layer in range(depth):
        i = layer // 2
        final = layer == depth - 1
        if layer % 2 == 0:
            x2 = _even_layer(x2, b, t, even_norm[i], even_w_in[i], even_b_glu[i], even_w_dw[i], even_b_dw[i],
                             even_conv_ln_g[i], even_conv_ln_b[i], even_w_pw[i], even_b_pw[i], even_w_out[i],
                             final_norm, final)
        else:
            x2 = _odd_layer(x2, b, t, odd_norm[i], odd_w_in[i], odd_q_norm[i], odd_w_qb[i], odd_kv_norm[i],
                            odd_w_uk[i], odd_w_uv[i], odd_w_iq[i], odd_ik_ln_g[i], odd_ik_ln_b[i],
                            odd_w_out[i], final_norm, final)
    return x2.reshape(b, t, d)
```

```python
import functools

import jax
import jax.numpy as jnp
from jax import lax
from jax.experimental import pallas as pl
from jax.experimental.pallas import tpu as pltpu

BF16 = jnp.bfloat16
F32 = jnp.float32
I32 = jnp.int32

EPS = 1e-6
LANES = 128
SUBLANES = 8
VMEM_LIMIT = 56 * 1024 * 1024

CONV_KERNEL = 31
CONV_HALO = 32
MOBA_HEAD_DIM = 128
MOBA_BLOCK = 256
MOBA_TOPK = 3
DSA_HEADS = 16
DSA_HEAD_DIM = 128
Q_LORA = 512
KV_LORA = 256
IDX_HEADS = 16
IDX_DIM = 64
DSA_TOPK_MAX = 256
DSA_TQ = 128
DSA_TK = 512
NEG = -1e30
LOG2E = 1.4426950408889634
INT_MIN = -2 ** 31
CODE_NEG_INF = (0xFF800000 ^ 0x7FFFFFFF) - 2 ** 32

NT_DIMS = (((1,), (1,)), ((), ()))


def _params(*sem):
    return pltpu.CompilerParams(dimension_semantics=sem, vmem_limit_bytes=VMEM_LIMIT)


def _rms(x, g):
    return x * lax.rsqrt(jnp.mean(x * x, axis=-1, keepdims=True) + EPS) * g


def _silu(x):
    return x * jax.nn.sigmoid(x)


def _norm_matmul_kernel(x_ref, g_ref, w_ref, o_ref, xn_ref):
    @pl.when(pl.program_id(1) == 0)
    def _():
        xn_ref[...] = _rms(x_ref[...], g_ref[...]).astype(BF16)

    o_ref[...] = jnp.dot(xn_ref[...], w_ref[...], preferred_element_type=F32).astype(o_ref.dtype)


def _norm_matmul(x2, g, w, tm, tn):
    n, d = x2.shape
    m = w.shape[1]
    return pl.pallas_call(
        _norm_matmul_kernel,
        grid=(n // tm, m // tn),
        in_specs=[
            pl.BlockSpec((tm, d), lambda i, j: (i, 0)),
            pl.BlockSpec((1, d), lambda i, j: (0, 0)),
            pl.BlockSpec((d, tn), lambda i, j: (0, j)),
        ],
        out_specs=pl.BlockSpec((tm, tn), lambda i, j: (i, j)),
        out_shape=jax.ShapeDtypeStruct((n, m), BF16),
        scratch_shapes=[pltpu.VMEM((tm, d), BF16)],
        compiler_params=_params("parallel", "arbitrary"),
        name="norm_matmul",
    )(x2, g.reshape(1, d), w)


def _proj_residual_kernel(n_in, final, *refs):
    x_ref = refs[0]
    a_refs = refs[1:1 + n_in]
    w_refs = refs[1 + n_in:1 + 2 * n_in]
    g_ref = refs[1 + 2 * n_in]
    o_ref = refs[2 + 2 * n_in]
    y = x_ref[...]
    for a_ref, w_ref in zip(a_refs, w_refs):
        y = y + jnp.dot(a_ref[...], w_ref[...], preferred_element_type=F32)
    if final:
        y = _rms(y, g_ref[...])
    o_ref[...] = y


def _proj_residual(x2, acts, ws, final_g, final, tm):
    n, d = x2.shape
    n_in = len(acts)
    in_specs = [pl.BlockSpec((tm, d), lambda i: (i, 0))]
    in_specs += [pl.BlockSpec((tm, a.shape[1]), lambda i: (i, 0)) for a in acts]
    in_specs += [pl.BlockSpec(w.shape, lambda i: (0, 0)) for w in ws]
    in_specs += [pl.BlockSpec((1, d), lambda i: (0, 0))]
    return pl.pallas_call(
        functools.partial(_proj_residual_kernel, n_in, final),
        grid=(n // tm,),
        in_specs=in_specs,
        out_specs=pl.BlockSpec((tm, d), lambda i: (i, 0)),
        out_shape=jax.ShapeDtypeStruct((n, d), F32),
        compiler_params=_params("parallel"),
        name="proj_residual",
    )(x2, *acts, *ws, final_g.reshape(1, d))


def _conv_kernel(tt, av_ref, ag_ref, gate_ref, hv_ref, hg_ref, bv_ref, bg_ref, wdw_ref, bdw_ref,
                 lng_ref, lnb_ref, wpw_ref, bpw_ref, o_ref, ubuf):
    i = pl.program_id(1)
    bv = bv_ref[...]
    bg = bg_ref[...]

    def glu(v, g):
        return (v.astype(F32) + bv) * jax.nn.sigmoid(g.astype(F32) + bg)

    ubuf[0:CONV_HALO, :] = jnp.where(i > 0, glu(hv_ref[...], hg_ref[...]), 0.0)
    ubuf[CONV_HALO:, :] = glu(av_ref[...], ag_ref[...])
    shift = CONV_HALO - (CONV_KERNEL - 1)
    acc = wdw_ref[0:1, :] * ubuf[shift:shift + tt, :]
    for j in range(1, CONV_KERNEL):
        acc = acc + wdw_ref[j:j + 1, :] * ubuf[shift + j:shift + j + tt, :]
    y = acc + bdw_ref[...]
    mu = jnp.mean(y, axis=-1, keepdims=True)
    yc = y - mu
    var = jnp.mean(yc * yc, axis=-1, keepdims=True)
    y = _silu(yc * lax.rsqrt(var + EPS) * lng_ref[...] + lnb_ref[...])
    y = jnp.dot(y.astype(BF16), wpw_ref[...], preferred_element_type=F32) + bpw_ref[...]
    o_ref[...] = (y * _silu(gate_ref[...].astype(F32))).astype(o_ref.dtype)


def _conv_branch(z3, b_glu, w_dw, b_dw, ln_g, ln_b, w_pw, b_pw, tt):
    b, t, _ = z3.shape
    c = w_pw.shape[0]
    hb = tt // CONV_HALO
    row = lambda v: v.reshape(1, c).astype(F32)
    w_dw_p = jnp.pad(w_dw.astype(F32), ((0, 32 - CONV_KERNEL), (0, 0)))
    const = lambda shape: pl.BlockSpec(shape, lambda bi, i: (0, 0))
    return pl.pallas_call(
        functools.partial(_conv_kernel, tt),
        grid=(b, t // tt),
        in_specs=[
            pl.BlockSpec((None, tt, c), lambda bi, i: (bi, i, 0)),
            pl.BlockSpec((None, tt, c), lambda bi, i: (bi, i, 1)),
            pl.BlockSpec((None, tt, c), lambda bi, i: (bi, i, 2)),
            pl.BlockSpec((None, CONV_HALO, c), lambda bi, i: (bi, jnp.maximum(i * hb - 1, 0), 0)),
            pl.BlockSpec((None, CONV_HALO, c), lambda bi, i: (bi, jnp.maximum(i * hb - 1, 0), 1)),
            const((1, c)), const((1, c)), const((32, c)), const((1, c)), const((1, c)), const((1, c)),
            const((c, c)), const((1, c)),
        ],
        out_specs=pl.BlockSpec((None, tt, c), lambda bi, i: (bi, i, 0)),
        out_shape=jax.ShapeDtypeStruct((b, t, c), BF16),
        scratch_shapes=[pltpu.VMEM((tt + CONV_HALO, c), F32)],
        compiler_params=_params("parallel", "parallel"),
        name="conv_branch",
    )(z3, z3, z3, z3, z3, row(b_glu[:c]), row(b_glu[c:]), w_dw_p, row(b_dw), row(ln_g), row(ln_b),
      w_pw, row(b_pw))


def _flash_update(s_chunks, m_old, l_old):
    m_new = jnp.maximum(m_old, jnp.max(functools.reduce(jnp.maximum, s_chunks), axis=1, keepdims=True))
    alpha = jnp.exp2(m_old - m_new)
    p = [jnp.exp2(s - m_new) for s in s_chunks]
    return m_new, alpha, alpha * l_old + functools.reduce(jnp.add, p), p


def _moba_kernel(nb, heads, q_ref, k_ref, v_ref, g_ref, o_ref, kmean_ref, qa_ref, m_ref, l_ref, acc_ref):
    i = pl.program_id(1)
    blk, dh = MOBA_BLOCK, MOBA_HEAD_DIM
    nch = blk // LANES
    scale = dh ** -0.5 * LOG2E
    nbp = kmean_ref.shape[1]
    head = lambda h: slice(h * dh, (h + 1) * dh)

    @pl.when(i == 0)
    def _():
        kmean_ref[...] = jnp.zeros_like(kmean_ref)
        for n in range(nb):
            mean = jnp.mean(k_ref[n * blk:(n + 1) * blk, :].astype(F32), axis=0, keepdims=True)
            for h in range(heads):
                kmean_ref[h, n:n + 1, :] = mean[:, head(h)]

    n_iota = lax.broadcasted_iota(I32, (nbp, blk), 0)
    past = n_iota < i
    for h in range(heads):
        gate_t = lax.dot_general(kmean_ref[h].astype(BF16), q_ref[:, head(h)], NT_DIMS,
                                 preferred_element_type=F32)
        gate_t = jnp.where(past, gate_t, -jnp.inf)
        rank = jnp.zeros((nbp, blk), I32)
        for jp in range(nb - 1):
            row = gate_t[jp:jp + 1, :]
            rank = rank + jnp.where(row > gate_t, 1, jnp.where(row == gate_t, jnp.where(jp < n_iota, 1, 0), 0))
        pen_t = jnp.where(past, jnp.where(rank < MOBA_TOPK, 0.0, NEG), NEG)
        pen_t = jnp.concatenate([pen_t, jnp.zeros((LANES - nbp, blk), F32)], axis=0)
        qa_ref[h] = jnp.concatenate([q_ref[:, head(h)], pen_t.T.astype(BF16)], axis=1)

    m_ref[...] = jnp.full_like(m_ref, NEG)
    l_ref[...] = jnp.zeros_like(l_ref)
    acc_ref[...] = jnp.zeros_like(acc_ref)

    def step(h, j, s):
        k0 = pl.multiple_of(j * blk, blk)
        s_chunks = [s[:, c * LANES:(c + 1) * LANES] for c in range(nch)]
        m_new, alpha, l_new, p = _flash_update(s_chunks, m_ref[h], l_ref[h])
        m_ref[h] = m_new
        l_ref[h] = l_new
        pv = jnp.dot(jnp.concatenate(p, axis=1).astype(BF16), v_ref[pl.ds(k0, blk), head(h)],
                     preferred_element_type=F32)
        acc_ref[h] = alpha * acc_ref[h] + pv

    k_lane = lax.broadcasted_iota(I32, (blk, LANES), 1)

    def past_step(j, _):
        k0 = pl.multiple_of(j * blk, blk)
        onehot = jnp.where(k_lane == j, 1.0, 0.0).astype(BF16)
        for h in range(heads):
            ka = jnp.concatenate([k_ref[pl.ds(k0, blk), head(h)], onehot], axis=1)
            step(h, j, lax.dot_general(qa_ref[h], ka, NT_DIMS, preferred_element_type=F32) * scale)
        return 0

    lax.fori_loop(0, i, past_step, 0)
    r = lax.broadcasted_iota(I32, (blk, blk), 0)
    c = lax.broadcasted_iota(I32, (blk, blk), 1)
    causal = jnp.where(c <= r, 0.0, NEG)
    k0 = pl.multiple_of(i * blk, blk)
    for h in range(heads):
        s = lax.dot_general(q_ref[:, head(h)], k_ref[pl.ds(k0, blk), head(h)], NT_DIMS,
                            preferred_element_type=F32) * scale
        step(h, i, s + causal)
        l = jnp.sum(l_ref[h], axis=1, keepdims=True)
        o_ref[:, head(h)] = (acc_ref[h] / l * _silu(g_ref[:, head(h)].astype(F32))).astype(o_ref.dtype)


def _moba_branch(z3, heads, col0):
    b, t, _ = z3.shape
    nb = t // MOBA_BLOCK
    w = heads * MOBA_HEAD_DIM
    c0 = col0 // w
    return pl.pallas_call(
        functools.partial(_moba_kernel, nb, heads),
        grid=(b, nb),
        in_specs=[
            pl.BlockSpec((None, MOBA_BLOCK, w), lambda bi, i: (bi, i, c0)),
            pl.BlockSpec((None, t, w), lambda bi, i: (bi, 0, c0 + 1)),
            pl.BlockSpec((None, t, w), lambda bi, i: (bi, 0, c0 + 2)),
            pl.BlockSpec((None, MOBA_BLOCK, w), lambda bi, i: (bi, i, c0 + 3)),
        ],
        out_specs=pl.BlockSpec((None, MOBA_BLOCK, w), lambda bi, i: (bi, i, 0)),
        out_shape=jax.ShapeDtypeStruct((b, t, w), BF16),
        scratch_shapes=[
            pltpu.VMEM((heads, -(-nb // SUBLANES) * SUBLANES, MOBA_HEAD_DIM), F32),
            pltpu.VMEM((heads, MOBA_BLOCK, MOBA_HEAD_DIM + LANES), BF16),
            pltpu.VMEM((heads, MOBA_BLOCK, LANES), F32),
            pltpu.VMEM((heads, MOBA_BLOCK, LANES), F32),
            pltpu.VMEM((heads, MOBA_BLOCK, MOBA_HEAD_DIM), F32),
        ],
        compiler_params=_params("parallel", "arbitrary"),
        name="moba_branch",
    )(z3, z3, z3, z3)


def _dsa_prep_kernel(x_ref, g_ref, ws_ref, qn_ref, wqb_ref, kvn_ref, wuk_ref, wiq_ref, ikg_ref, ikb_ref,
                     qlat_ref, iq_ref, ckv_ref, ik2_ref, iw_ref):
    scale = DSA_HEAD_DIM ** -0.5 * LOG2E
    xn = _rms(x_ref[...], g_ref[...]).astype(BF16)
    zs = jnp.dot(xn, ws_ref[...], preferred_element_type=F32)
    o_kv = Q_LORA
    o_ik = Q_LORA + KV_LORA
    o_iw = o_ik + 2 * LANES
    cqn = _rms(zs[:, :Q_LORA], qn_ref[...]).astype(BF16)
    q = jnp.dot(cqn, wqb_ref[...], preferred_element_type=F32).astype(BF16)
    for h in range(DSA_HEADS):
        ql = jnp.dot(q[:, h * DSA_HEAD_DIM:(h + 1) * DSA_HEAD_DIM], wuk_ref[h], preferred_element_type=F32)
        qlat_ref[h] = (ql * scale).astype(qlat_ref.dtype)
    iq = jnp.dot(cqn, wiq_ref[...], preferred_element_type=F32)
    for p in range(IDX_HEADS // 2):
        iq_ref[p] = iq[:, p * LANES:(p + 1) * LANES].astype(iq_ref.dtype)
    ckv_ref[...] = _rms(zs[:, o_kv:o_kv + KV_LORA], kvn_ref[...]).astype(ckv_ref.dtype)
    lane = lax.broadcasted_iota(I32, (1, LANES), 1)
    for half in range(2):
        z = zs[:, o_ik + half * LANES:o_ik + (half + 1) * LANES]
        valid = (lane < IDX_DIM) if half == 0 else (lane >= IDX_DIM)
        mu = jnp.sum(z, axis=-1, keepdims=True) * (1.0 / IDX_DIM)
        zc = jnp.where(valid, z - mu, 0.0)
        var = jnp.sum(zc * zc, axis=-1, keepdims=True) * (1.0 / IDX_DIM)
        y = zc * lax.rsqrt(var + EPS) * ikg_ref[:, half * LANES:(half + 1) * LANES] \
            + ikb_ref[:, half * LANES:(half + 1) * LANES]
        ik2_ref[:, half * LANES:(half + 1) * LANES] = y.astype(ik2_ref.dtype)
    iw_ref[...] = zs[:, o_iw:o_iw + LANES] * (IDX_HEADS ** -0.5 * IDX_DIM ** -0.5)


def _dsa_prep(x3, norm_g, w_small, q_norm, w_qb, kv_norm, w_uk, w_iq, ik_g2, ik_b2, tm):
    b, t, d = x3.shape
    const = lambda shape: pl.BlockSpec(shape, lambda bi, i: (0,) * len(shape))
    return pl.pallas_call(
        _dsa_prep_kernel,
        grid=(b, t // tm),
        in_specs=[
            pl.BlockSpec((None, tm, d), lambda bi, i: (bi, i, 0)),
            const((1, d)), const(w_small.shape), const((1, Q_LORA)), const(w_qb.shape),
            const((1, KV_LORA)), const(w_uk.shape), const(w_iq.shape), const((1, 2 * LANES)),
            const((1, 2 * LANES)),
        ],
        out_specs=[
            pl.BlockSpec((None, DSA_HEADS, tm, KV_LORA), lambda bi, i: (bi, 0, i, 0)),
            pl.BlockSpec((None, IDX_HEADS // 2, tm, LANES), lambda bi, i: (bi, 0, i, 0)),
            pl.BlockSpec((None, tm, KV_LORA), lambda bi, i: (bi, i, 0)),
            pl.BlockSpec((None, tm, 2 * LANES), lambda bi, i: (bi, i, 0)),
            pl.BlockSpec((None, tm, LANES), lambda bi, i: (bi, i, 0)),
        ],
        out_shape=[
            jax.ShapeDtypeStruct((b, DSA_HEADS, t, KV_LORA), BF16),
            jax.ShapeDtypeStruct((b, IDX_HEADS // 2, t, LANES), BF16),
            jax.ShapeDtypeStruct((b, t, KV_LORA), BF16),
            jax.ShapeDtypeStruct((b, t, 2 * LANES), BF16),
            jax.ShapeDtypeStruct((b, t, LANES), F32),
        ],
        compiler_params=_params("parallel", "parallel"),
        name="dsa_prep",
    )(x3, norm_g.reshape(1, d), w_small, q_norm.reshape(1, Q_LORA), w_qb, kv_norm.reshape(1, KV_LORA),
      w_uk, w_iq, ik_g2, ik_b2)


def _dsa_attn_kernel(k_top, t_len, iq_ref, iw_ref, ik2_ref, qlat_ref, ckv_ref, gate_ref, wuv_ref, o_ref,
                     sc_ref, m_ref, l_ref, alpha_ref, p_ref, acc_ref):
    tq, tk = DSA_TQ, DSA_TK
    sub = SUBLANES
    i = pl.program_id(1)
    nkb = ((i + 1) * tq + tk - 1) // tk
    qpos8 = i * tq + lax.broadcasted_iota(I32, (sub, tq), 1)
    row8 = lax.broadcasted_iota(I32, (sub, tq), 0)

    def count(indicator):
        def body(j, cnt):
            k0 = pl.multiple_of(j * tk, tk)
            parts = [cnt, jnp.zeros_like(cnt), jnp.zeros_like(cnt), jnp.zeros_like(cnt)]
            for g in range(tk // sub):
                kb = sc_ref[pl.ds(k0 + g * sub, sub), :]
                parts[g % 4] = parts[g % 4] + indicator(kb, k0 + g * sub + row8)
            return (parts[0] + parts[1]) + (parts[2] + parts[3])

        cnt = lax.fori_loop(0, nkb, body, jnp.zeros((sub, tq), I32)).astype(F32)
        return jnp.broadcast_to(jnp.sum(cnt, axis=0, keepdims=True), (sub, tq))

    iq_all = iq_ref[...].reshape(IDX_HEADS // 2 * tq, LANES)
    iw_t = iw_ref[...].T
    iw8 = [jnp.broadcast_to(iw_t[h:h + 1, :], (sub, tq)) for h in range(IDX_HEADS)]

    def score_blk(j, _):
        k0 = pl.multiple_of(j * tk, tk)
        se = lax.dot_general(ik2_ref[pl.ds(k0, tk), 0:LANES], iq_all, NT_DIMS, preferred_element_type=F32)
        so = lax.dot_general(ik2_ref[pl.ds(k0, tk), LANES:2 * LANES], iq_all, NT_DIMS,
                             preferred_element_type=F32)
        for g in range(tk // sub):
            rows = slice(g * sub, (g + 1) * sub)
            sc = jnp.zeros((sub, tq), F32)
            for p in range(IDX_HEADS // 2):
                cols = slice(p * tq, (p + 1) * tq)
                sc = sc + jnp.maximum(se[rows, cols], 0.0) * iw8[2 * p]
                sc = sc + jnp.maximum(so[rows, cols], 0.0) * iw8[2 * p + 1]
            sc_ref[pl.ds(k0 + g * sub, sub), :] = jnp.where(k0 + g * sub + row8 <= qpos8, sc, -jnp.inf)
        return 0

    lax.fori_loop(0, nkb, score_blk, 0)

    def decode(code):
        bits = code ^ ((code >> 31) & 0x7FFFFFFF)
        return jnp.where(code < CODE_NEG_INF, -jnp.inf, pltpu.bitcast(bits, F32))

    def bit_body(b, ans):
        cand = ans + lax.shift_left(jnp.int32(1), 31 - b)
        cand_f = decode(cand)
        return jnp.where(count(lambda sb, row: jnp.where(sb >= cand_f, 1, 0)) >= k_top, cand, ans)

    thr = decode(lax.fori_loop(0, 32, bit_body, jnp.full((sub, tq), INT_MIN, I32)))

    n_ge = count(lambda kb, row: jnp.where(kb >= thr, 1, 0))
    need = k_top - count(lambda kb, row: jnp.where(kb > thr, 1, 0))
    excess = jnp.max(jnp.where(n_ge > k_top, 1.0, 0.0)) > 0.5
    idx_bits = (t_len - 1).bit_length()

    def tie_cut():
        def idx_body(b, d):
            cand = d + lax.shift_left(jnp.int32(1), idx_bits - 1 - b)
            below = count(lambda kb, row: jnp.where(kb == thr, jnp.where(row < cand, 1, 0), 0))
            return jnp.where(below < need, cand, d)

        d = lax.fori_loop(0, idx_bits, idx_body, jnp.zeros((sub, tq), I32))
        return jnp.where(n_ge > k_top, d, t_len)

    cut = lax.cond(excess, tie_cut, lambda: jnp.full((sub, tq), t_len, I32))

    nch = tk // LANES
    q_all = qlat_ref[...].reshape(DSA_HEADS * tq, KV_LORA)
    m_ref[...] = jnp.full_like(m_ref, NEG)
    l_ref[...] = jnp.zeros_like(l_ref)
    acc_ref[...] = jnp.zeros_like(acc_ref)
    thr_b = jnp.broadcast_to(thr[0:1, :], (LANES, tq))
    cut_b = jnp.broadcast_to(cut[0:1, :], (LANES, tq))
    qpos_b = i * tq + lax.broadcasted_iota(I32, (LANES, tq), 1)
    row_b = lax.broadcasted_iota(I32, (LANES, tq), 0)

    def attn_blk(j, _):
        k0 = pl.multiple_of(j * tk, tk)
        kv = ckv_ref[pl.ds(k0, tk), :]
        s_all = lax.dot_general(q_all, kv, NT_DIMS, preferred_element_type=F32)
        bias = []
        for c in range(nch):
            kb = sc_ref[pl.ds(k0 + c * LANES, LANES), :]
            kpos = k0 + c * LANES + row_b
            tie = jnp.where(kb == thr_b, jnp.where(kpos <= cut_b, 0.0, NEG), NEG)
            bias.append(jnp.where(kpos <= qpos_b, jnp.where(kb > thr_b, 0.0, tie), NEG).T)
        for h in range(DSA_HEADS):
            rows = slice(h * tq, (h + 1) * tq)
            s_chunks = [s_all[rows, c * LANES:(c + 1) * LANES] + bias[c] for c in range(nch)]
            m_new, alpha, l_new, p = _flash_update(s_chunks, m_ref[rows], l_ref[rows])
            m_ref[rows] = m_new
            l_ref[rows] = l_new
            alpha_ref[rows] = alpha
            p_ref[rows] = jnp.concatenate(p, axis=1).astype(BF16)
        alpha = alpha_ref[...]
        acc_ref[...] = jnp.concatenate([alpha] * (KV_LORA // LANES), axis=1) * acc_ref[...] \
            + jnp.dot(p_ref[...], kv, preferred_element_type=F32)
        return 0

    lax.fori_loop(0, nkb, attn_blk, 0)

    for h in range(DSA_HEADS):
        rows = slice(h * tq, (h + 1) * tq)
        o_lat = (acc_ref[rows] / jnp.sum(l_ref[rows], axis=1, keepdims=True)).astype(BF16)
        o = jnp.dot(o_lat, wuv_ref[h], preferred_element_type=F32)
        cols = slice(h * DSA_HEAD_DIM, (h + 1) * DSA_HEAD_DIM)
        o_ref[:, cols] = (o * _silu(gate_ref[:, cols].astype(F32))).astype(o_ref.dtype)


def _dsa_attn(iq, iw, ik2, qlat, ckv, gate3, w_uv):
    b, t, _ = ckv.shape
    k_top = min(DSA_TOPK_MAX, t // 4)
    tq = DSA_TQ
    width = DSA_HEADS * DSA_HEAD_DIM
    return pl.pallas_call(
        functools.partial(_dsa_attn_kernel, k_top, t),
        grid=(b, t // tq),
        in_specs=[
            pl.BlockSpec((None, IDX_HEADS // 2, tq, LANES), lambda bi, i: (bi, 0, i, 0)),
            pl.BlockSpec((None, tq, LANES), lambda bi, i: (bi, i, 0)),
            pl.BlockSpec((None, t, 2 * LANES), lambda bi, i: (bi, 0, 0)),
            pl.BlockSpec((None, DSA_HEADS, tq, KV_LORA), lambda bi, i: (bi, 0, i, 0)),
            pl.BlockSpec((None, t, KV_LORA), lambda bi, i: (bi, 0, 0)),
            pl.BlockSpec((None, tq, width), lambda bi, i: (bi, i, 0)),
            pl.BlockSpec(w_uv.shape, lambda bi, i: (0, 0, 0)),
        ],
        out_specs=pl.BlockSpec((None, tq, width), lambda bi, i: (bi, i, 0)),
        out_shape=jax.ShapeDtypeStruct((b, t, width), BF16),
        scratch_shapes=[
            pltpu.VMEM((t, tq), F32),
            pltpu.VMEM((DSA_HEADS * tq, LANES), F32),
            pltpu.VMEM((DSA_HEADS * tq, LANES), F32),
            pltpu.VMEM((DSA_HEADS * tq, LANES), F32),
            pltpu.VMEM((DSA_HEADS * tq, DSA_TK), BF16),
            pltpu.VMEM((DSA_HEADS * tq, KV_LORA), F32),
        ],
        compiler_params=_params("parallel", "parallel"),
        name="dsa_attn",
    )(iq, iw, ik2, qlat, ckv, gate3, w_uv)


def _even_layer(x2, b, t, norm_g, w_in, b_glu, w_dw, b_dw, ln_g, ln_b, w_pw, b_pw, w_out, final_g, final):
    c = w_pw.shape[0]
    heads = (w_in.shape[1] - 3 * c) // (4 * MOBA_HEAD_DIM)
    z = _norm_matmul(x2, norm_g, w_in.astype(BF16), tm=1024, tn=512)
    z3 = z.reshape(b, t, z.shape[1])
    ya = _conv_branch(z3, b_glu, w_dw, b_dw, ln_g, ln_b, w_pw.astype(BF16), b_pw, tt=256)
    yb = _moba_branch(z3, heads, 3 * c)
    w_out = w_out.astype(BF16)
    return _proj_residual(x2, [ya.reshape(b * t, c), yb.reshape(b * t, -1)], [w_out[:c], w_out[c:]],
                          final_g, final, tm=512)


def _odd_layer(x2, b, t, norm_g, w_in, q_norm, w_qb, kv_norm, w_uk, w_uv, w_iq, ik_g, ik_b, w_out,
               final_g, final):
    d = x2.shape[1]
    o_ik = Q_LORA + KV_LORA
    o_iw = o_ik + IDX_DIM
    o_gate = o_iw + IDX_HEADS
    zeros = lambda n: jnp.zeros((d, n), w_in.dtype)
    w_ik = w_in[:, o_ik:o_iw]
    w_small = jnp.concatenate(
        [w_in[:, :o_ik], w_ik, zeros(LANES - IDX_DIM), zeros(LANES - IDX_DIM), w_ik,
         w_in[:, o_iw:o_gate], zeros(LANES - IDX_HEADS)], axis=1).astype(BF16)
    pad = jnp.zeros((IDX_DIM,), F32)
    ik_g2 = jnp.concatenate([ik_g, pad, pad, ik_g]).reshape(1, 2 * LANES)
    ik_b2 = jnp.concatenate([ik_b, pad, pad, ik_b]).reshape(1, 2 * LANES)
    x3 = x2.reshape(b, t, d)
    qlat, iq, ckv, ik2, iw = _dsa_prep(x3, norm_g, w_small, q_norm, w_qb.astype(BF16), kv_norm,
                                       w_uk.astype(BF16), w_iq.astype(BF16), ik_g2, ik_b2, tm=256)
    gate = _norm_matmul(x2, norm_g, w_in[:, o_gate:].astype(BF16), tm=1024, tn=512)
    og = _dsa_attn(iq, iw, ik2, qlat, ckv, gate.reshape(b, t, -1), w_uv.astype(BF16))
    return _proj_residual(x2, [og.reshape(b * t, -1)], [w_out.astype(BF16)], final_g, final, tm=512)


def kernel(x, even_norm, even_w_in, even_b_glu, even_w_dw, even_b_dw, even_conv_ln_g, even_conv_ln_b,
           even_w_pw, even_b_pw, even_w_out, odd_norm, odd_w_in, odd_q_norm, odd_w_qb, odd_kv_norm,
           odd_w_uk, odd_w_uv, odd_w_iq, odd_ik_ln_g, odd_ik_ln_b, odd_w_out, final_norm):
    b, t, d = x.shape
    depth = even_norm.shape[0] + odd_norm.shape[0]
    x2 = x.reshape(b * t, d)
    for layer in range(depth):
        i = layer // 2
        final = layer == depth - 1
        if layer % 2 == 0:
            x2 = _even_layer(x2, b, t, even_norm[i], even_w_in[i], even_b_glu[i], even_w_dw[i], even_b_dw[i],
                             even_conv_ln_g[i], even_conv_ln_b[i], even_w_pw[i], even_b_pw[i], even_w_out[i],
                             final_norm, final)
        else:
            x2 = _odd_layer(x2, b, t, odd_norm[i], odd_w_in[i], odd_q_norm[i], odd_w_qb[i], odd_kv_norm[i],
                            odd_w_uk[i], odd_w_uv[i], odd_w_iq[i], odd_ik_ln_g[i], odd_ik_ln_b[i],
                            odd_w_out[i], final_norm, final)
    return x2.reshape(b, t, d)
```

```python
import functools

import jax
import jax.numpy as jnp
from jax import lax
from jax.experimental import pallas as pl
from jax.experimental.pallas import tpu as pltpu

BF16 = jnp.bfloat16
F32 = jnp.float32
I32 = jnp.int32

EPS = 1e-6
LANES = 128
SUBLANES = 8
VMEM_LIMIT = 56 * 1024 * 1024

CONV_KERNEL = 31
CONV_HALO = 32
CONV_ROWS = 64
MOBA_HEAD_DIM = 128
MOBA_BLOCK = 256
MOBA_TOPK = 3
DSA_HEADS = 16
DSA_HEAD_DIM = 128
Q_LORA = 512
KV_LORA = 256
IDX_HEADS = 16
IDX_DIM = 64
DSA_TOPK_MAX = 256
DSA_TQ = 128
DSA_TK = 512
DSA_HEAD_GROUP = 4
NEG = -1e30
LOG2E = 1.4426950408889634
INT_MIN = -2 ** 31
CODE_NEG_INF = (0xFF800000 ^ 0x7FFFFFFF) - 2 ** 32

NT_DIMS = (((1,), (1,)), ((), ()))


def _params(*sem):
    return pltpu.CompilerParams(dimension_semantics=sem, vmem_limit_bytes=VMEM_LIMIT)


def _rms(x, g):
    return x * lax.rsqrt(jnp.mean(x * x, axis=-1, keepdims=True) + EPS) * g


def _silu(x):
    return x * jax.nn.sigmoid(x)


def _norm_matmul_kernel(x_ref, g_ref, w_ref, o_ref, xn_ref):
    @pl.when(pl.program_id(1) == 0)
    def _():
        xn_ref[...] = _rms(x_ref[...], g_ref[...]).astype(BF16)

    o_ref[...] = jnp.dot(xn_ref[...], w_ref[...], preferred_element_type=F32).astype(o_ref.dtype)


def _norm_matmul(x2, g, w, tm, tn):
    n, d = x2.shape
    m = w.shape[1]
    return pl.pallas_call(
        _norm_matmul_kernel,
        grid=(n // tm, m // tn),
        in_specs=[
            pl.BlockSpec((tm, d), lambda i, j: (i, 0)),
            pl.BlockSpec((1, d), lambda i, j: (0, 0)),
            pl.BlockSpec((d, tn), lambda i, j: (0, j)),
        ],
        out_specs=pl.BlockSpec((tm, tn), lambda i, j: (i, j)),
        out_shape=jax.ShapeDtypeStruct((n, m), BF16),
        scratch_shapes=[pltpu.VMEM((tm, d), BF16)],
        compiler_params=_params("parallel", "arbitrary"),
        name="norm_matmul",
    )(x2, g.reshape(1, d), w)


def _proj_residual_kernel(n_in, final, *refs):
    x_ref = refs[0]
    a_refs = refs[1:1 + n_in]
    w_refs = refs[1 + n_in:1 + 2 * n_in]
    g_ref = refs[1 + 2 * n_in]
    o_ref = refs[2 + 2 * n_in]
    y = x_ref[...]
    for a_ref, w_ref in zip(a_refs, w_refs):
        y = y + jnp.dot(a_ref[...], w_ref[...], preferred_element_type=F32)
    if final:
        y = _rms(y, g_ref[...])
    o_ref[...] = y


def _proj_residual(x2, acts, ws, final_g, final, tm):
    n, d = x2.shape
    n_in = len(acts)
    in_specs = [pl.BlockSpec((tm, d), lambda i: (i, 0))]
    in_specs += [pl.BlockSpec((tm, a.shape[1]), lambda i: (i, 0)) for a in acts]
    in_specs += [pl.BlockSpec(w.shape, lambda i: (0, 0)) for w in ws]
    in_specs += [pl.BlockSpec((1, d), lambda i: (0, 0))]
    return pl.pallas_call(
        functools.partial(_proj_residual_kernel, n_in, final),
        grid=(n // tm,),
        in_specs=in_specs,
        out_specs=pl.BlockSpec((tm, d), lambda i: (i, 0)),
        out_shape=jax.ShapeDtypeStruct((n, d), F32),
        compiler_params=_params("parallel"),
        name="proj_residual",
    )(x2, *acts, *ws, final_g.reshape(1, d))


def _conv_kernel(tt, av_ref, ag_ref, gate_ref, hv_ref, hg_ref, bv_ref, bg_ref, wdw_ref, bdw_ref,
                 lng_ref, lnb_ref, wpw_ref, bpw_ref, o_ref, ubuf, shifted, ybuf):
    i = pl.program_id(1)
    bv = bv_ref[...]
    bg = bg_ref[...]

    def glu(v, g):
        return (v.astype(F32) + bv) * jax.nn.sigmoid(g.astype(F32) + bg)

    ubuf[0:CONV_HALO, :] = jnp.where(i > 0, glu(hv_ref[...], hg_ref[...]), 0.0)
    ubuf[CONV_HALO:, :] = glu(av_ref[...], ag_ref[...])
    span = tt + CONV_HALO - SUBLANES
    for b in range(1, SUBLANES):
        shifted[b - 1, 0:span, :] = ubuf[b:b + span, :]
    shift = CONV_HALO - (CONV_KERNEL - 1)
    c = ubuf.shape[1]
    for r0 in range(0, tt, CONV_ROWS):
        for c0 in range(0, c, LANES):
            acc = jnp.broadcast_to(bdw_ref[:, c0:c0 + LANES], (CONV_ROWS, LANES))
            for j in range(CONV_KERNEL):
                a, b = divmod(shift + j, SUBLANES)
                rows = slice(r0 + a * SUBLANES, r0 + a * SUBLANES + CONV_ROWS)
                src = ubuf[rows, c0:c0 + LANES] if b == 0 else shifted[b - 1, rows, c0:c0 + LANES]
                acc = acc + wdw_ref[j:j + 1, c0:c0 + LANES] * src
            ybuf[r0:r0 + CONV_ROWS, c0:c0 + LANES] = acc
    y = ybuf[...]
    mu = jnp.mean(y, axis=-1, keepdims=True)
    yc = y - mu
    var = jnp.mean(yc * yc, axis=-1, keepdims=True)
    y = _silu(yc * lax.rsqrt(var + EPS) * lng_ref[...] + lnb_ref[...])
    y = jnp.dot(y.astype(BF16), wpw_ref[...], preferred_element_type=F32) + bpw_ref[...]
    o_ref[...] = (y * _silu(gate_ref[...].astype(F32))).astype(o_ref.dtype)


def _conv_branch(z3, b_glu, w_dw, b_dw, ln_g, ln_b, w_pw, b_pw, tt):
    b, t, _ = z3.shape
    c = w_pw.shape[0]
    hb = tt // CONV_HALO
    row = lambda v: v.reshape(1, c).astype(F32)
    w_dw_p = jnp.pad(w_dw.astype(F32), ((0, 32 - CONV_KERNEL), (0, 0)))
    const = lambda shape: pl.BlockSpec(shape, lambda bi, i: (0, 0))
    return pl.pallas_call(
        functools.partial(_conv_kernel, tt),
        grid=(b, t // tt),
        in_specs=[
            pl.BlockSpec((None, tt, c), lambda bi, i: (bi, i, 0)),
            pl.BlockSpec((None, tt, c), lambda bi, i: (bi, i, 1)),
            pl.BlockSpec((None, tt, c), lambda bi, i: (bi, i, 2)),
            pl.BlockSpec((None, CONV_HALO, c), lambda bi, i: (bi, jnp.maximum(i * hb - 1, 0), 0)),
            pl.BlockSpec((None, CONV_HALO, c), lambda bi, i: (bi, jnp.maximum(i * hb - 1, 0), 1)),
            const((1, c)), const((1, c)), const((32, c)), const((1, c)), const((1, c)), const((1, c)),
            const((c, c)), const((1, c)),
        ],
        out_specs=pl.BlockSpec((None, tt, c), lambda bi, i: (bi, i, 0)),
        out_shape=jax.ShapeDtypeStruct((b, t, c), BF16),
        scratch_shapes=[
            pltpu.VMEM((tt + CONV_HALO, c), F32),
            pltpu.VMEM((SUBLANES - 1, tt + CONV_HALO, c), F32),
            pltpu.VMEM((tt, c), F32),
        ],
        compiler_params=_params("parallel", "parallel"),
        name="conv_branch",
    )(z3, z3, z3, z3, z3, row(b_glu[:c]), row(b_glu[c:]), w_dw_p, row(b_dw), row(ln_g), row(ln_b),
      w_pw, row(b_pw))


def _flash_update(s_chunks, m_old, l_old):
    m_new = jnp.maximum(m_old, jnp.max(functools.reduce(jnp.maximum, s_chunks), axis=1, keepdims=True))
    alpha = jnp.exp2(m_old - m_new)
    p = [jnp.exp2(s - m_new) for s in s_chunks]
    return m_new, alpha, alpha * l_old + functools.reduce(jnp.add, p), p


def _moba_kernel(nb, heads, q_ref, k_ref, v_ref, g_ref, o_ref, kmean_ref, qa_ref, m_ref, l_ref, acc_ref):
    i = pl.program_id(1)
    blk, dh = MOBA_BLOCK, MOBA_HEAD_DIM
    nch = blk // LANES
    scale = dh ** -0.5 * LOG2E
    nbp = kmean_ref.shape[1]
    head = lambda h: slice(h * dh, (h + 1) * dh)

    @pl.when(i == 0)
    def _():
        kmean_ref[...] = jnp.zeros_like(kmean_ref)
        for n in range(nb):
            mean = jnp.mean(k_ref[n * blk:(n + 1) * blk, :].astype(F32), axis=0, keepdims=True)
            for h in range(heads):
                kmean_ref[h, n:n + 1, :] = mean[:, head(h)]

    n_iota = lax.broadcasted_iota(I32, (nbp, blk), 0)
    past = n_iota < i
    for h in range(heads):
        gate_t = lax.dot_general(kmean_ref[h].astype(BF16), q_ref[:, head(h)], NT_DIMS,
                                 preferred_element_type=F32)
        gate_t = jnp.where(past, gate_t, -jnp.inf)
        rank = jnp.zeros((nbp, blk), I32)
        for jp in range(nb - 1):
            row = gate_t[jp:jp + 1, :]
            rank = rank + jnp.where(row > gate_t, 1, jnp.where(row == gate_t, jnp.where(jp < n_iota, 1, 0), 0))
        pen_t = jnp.where(past, jnp.where(rank < MOBA_TOPK, 0.0, NEG), NEG)
        pen_t = jnp.concatenate([pen_t, jnp.zeros((LANES - nbp, blk), F32)], axis=0)
        qa_ref[h] = jnp.concatenate([q_ref[:, head(h)], pen_t.T.astype(BF16)], axis=1)

    m_ref[...] = jnp.full_like(m_ref, NEG)
    l_ref[...] = jnp.zeros_like(l_ref)
    acc_ref[...] = jnp.zeros_like(acc_ref)

    def step(h, j, s):
        k0 = pl.multiple_of(j * blk, blk)
        s_chunks = [s[:, c * LANES:(c + 1) * LANES] for c in range(nch)]
        m_new, alpha, l_new, p = _flash_update(s_chunks, m_ref[h], l_ref[h])
        m_ref[h] = m_new
        l_ref[h] = l_new
        pv = jnp.dot(jnp.concatenate(p, axis=1).astype(BF16), v_ref[pl.ds(k0, blk), head(h)],
                     preferred_element_type=F32)
        acc_ref[h] = alpha * acc_ref[h] + pv

    k_lane = lax.broadcasted_iota(I32, (blk, LANES), 1)

    def past_step(j):
        k0 = pl.multiple_of(j * blk, blk)
        onehot = jnp.where(k_lane == j, 1.0, 0.0).astype(BF16)
        for h in range(heads):
            ka = jnp.concatenate([k_ref[pl.ds(k0, blk), head(h)], onehot], axis=1)
            step(h, j, lax.dot_general(qa_ref[h], ka, NT_DIMS, preferred_element_type=F32) * scale)

    def past_pair(jj, _):
        past_step(2 * jj)
        past_step(2 * jj + 1)
        return 0

    lax.fori_loop(0, i // 2, past_pair, 0)

    @pl.when(i % 2 == 1)
    def _():
        past_step(i - 1)

    r = lax.broadcasted_iota(I32, (blk, blk), 0)
    c = lax.broadcasted_iota(I32, (blk, blk), 1)
    causal = jnp.where(c <= r, 0.0, NEG)
    k0 = pl.multiple_of(i * blk, blk)
    for h in range(heads):
        s = lax.dot_general(q_ref[:, head(h)], k_ref[pl.ds(k0, blk), head(h)], NT_DIMS,
                            preferred_element_type=F32) * scale
        step(h, i, s + causal)
        l = jnp.sum(l_ref[h], axis=1, keepdims=True)
        o_ref[:, head(h)] = (acc_ref[h] / l * _silu(g_ref[:, head(h)].astype(F32))).astype(o_ref.dtype)


def _moba_branch(z3, heads, col0):
    b, t, _ = z3.shape
    nb = t // MOBA_BLOCK
    w = heads * MOBA_HEAD_DIM
    c0 = col0 // w
    return pl.pallas_call(
        functools.partial(_moba_kernel, nb, heads),
        grid=(b, nb),
        in_specs=[
            pl.BlockSpec((None, MOBA_BLOCK, w), lambda bi, i: (bi, i, c0)),
            pl.BlockSpec((None, t, w), lambda bi, i: (bi, 0, c0 + 1)),
            pl.BlockSpec((None, t, w), lambda bi, i: (bi, 0, c0 + 2)),
            pl.BlockSpec((None, MOBA_BLOCK, w), lambda bi, i: (bi, i, c0 + 3)),
        ],
        out_specs=pl.BlockSpec((None, MOBA_BLOCK, w), lambda bi, i: (bi, i, 0)),
        out_shape=jax.ShapeDtypeStruct((b, t, w), BF16),
        scratch_shapes=[
            pltpu.VMEM((heads, -(-nb // SUBLANES) * SUBLANES, MOBA_HEAD_DIM), F32),
            pltpu.VMEM((heads, MOBA_BLOCK, MOBA_HEAD_DIM + LANES), BF16),
            pltpu.VMEM((heads, MOBA_BLOCK, LANES), F32),
            pltpu.VMEM((heads, MOBA_BLOCK, LANES), F32),
            pltpu.VMEM((heads, MOBA_BLOCK, MOBA_HEAD_DIM), F32),
        ],
        compiler_params=_params("parallel", "arbitrary"),
        name="moba_branch",
    )(z3, z3, z3, z3)


def _dsa_prep_kernel(x_ref, g_ref, ws_ref, qn_ref, wqb_ref, kvn_ref, wuk_ref, wiq_ref, ikg_ref, ikb_ref,
                     qlat_ref, iq_ref, ckv_ref, ik2_ref, iw_ref):
    scale = DSA_HEAD_DIM ** -0.5 * LOG2E
    xn = _rms(x_ref[...], g_ref[...]).astype(BF16)
    zs = jnp.dot(xn, ws_ref[...], preferred_element_type=F32)
    o_kv = Q_LORA
    o_ik = Q_LORA + KV_LORA
    o_iw = o_ik + 2 * LANES
    cqn = _rms(zs[:, :Q_LORA], qn_ref[...]).astype(BF16)
    q = jnp.dot(cqn, wqb_ref[...], preferred_element_type=F32).astype(BF16)
    for h in range(DSA_HEADS):
        ql = jnp.dot(q[:, h * DSA_HEAD_DIM:(h + 1) * DSA_HEAD_DIM], wuk_ref[h], preferred_element_type=F32)
        qlat_ref[h] = (ql * scale).astype(qlat_ref.dtype)
    iq = jnp.dot(cqn, wiq_ref[...], preferred_element_type=F32)
    for p in range(IDX_HEADS // 2):
        iq_ref[p] = iq[:, p * LANES:(p + 1) * LANES].astype(iq_ref.dtype)
    ckv_ref[...] = _rms(zs[:, o_kv:o_kv + KV_LORA], kvn_ref[...]).astype(ckv_ref.dtype)
    lane = lax.broadcasted_iota(I32, (1, LANES), 1)
    for half in range(2):
        z = zs[:, o_ik + half * LANES:o_ik + (half + 1) * LANES]
        valid = (lane < IDX_DIM) if half == 0 else (lane >= IDX_DIM)
        mu = jnp.sum(z, axis=-1, keepdims=True) * (1.0 / IDX_DIM)
        zc = jnp.where(valid, z - mu, 0.0)
        var = jnp.sum(zc * zc, axis=-1, keepdims=True) * (1.0 / IDX_DIM)
        y = zc * lax.rsqrt(var + EPS) * ikg_ref[:, half * LANES:(half + 1) * LANES] \
            + ikb_ref[:, half * LANES:(half + 1) * LANES]
        ik2_ref[:, half * LANES:(half + 1) * LANES] = y.astype(ik2_ref.dtype)
    iw_ref[...] = zs[:, o_iw:o_iw + LANES] * (IDX_HEADS ** -0.5 * IDX_DIM ** -0.5)


def _dsa_prep(x3, norm_g, w_small, q_norm, w_qb, kv_norm, w_uk, w_iq, ik_g2, ik_b2, tm):
    b, t, d = x3.shape
    const = lambda shape: pl.BlockSpec(shape, lambda bi, i: (0,) * len(shape))
    return pl.pallas_call(
        _dsa_prep_kernel,
        grid=(b, t // tm),
        in_specs=[
            pl.BlockSpec((None, tm, d), lambda bi, i: (bi, i, 0)),
            const((1, d)), const(w_small.shape), const((1, Q_LORA)), const(w_qb.shape),
            const((1, KV_LORA)), const(w_uk.shape), const(w_iq.shape), const((1, 2 * LANES)),
            const((1, 2 * LANES)),
        ],
        out_specs=[
            pl.BlockSpec((None, DSA_HEADS, tm, KV_LORA), lambda bi, i: (bi, 0, i, 0)),
            pl.BlockSpec((None, IDX_HEADS // 2, tm, LANES), lambda bi, i: (bi, 0, i, 0)),
            pl.BlockSpec((None, tm, KV_LORA), lambda bi, i: (bi, i, 0)),
            pl.BlockSpec((None, tm, 2 * LANES), lambda bi, i: (bi, i, 0)),
            pl.BlockSpec((None, tm, LANES), lambda bi, i: (bi, i, 0)),
        ],
        out_shape=[
            jax.ShapeDtypeStruct((b, DSA_HEADS, t, KV_LORA), BF16),
            jax.ShapeDtypeStruct((b, IDX_HEADS // 2, t, LANES), BF16),
            jax.ShapeDtypeStruct((b, t, KV_LORA), BF16),
            jax.ShapeDtypeStruct((b, t, 2 * LANES), BF16),
            jax.ShapeDtypeStruct((b, t, LANES), F32),
        ],
        compiler_params=_params("parallel", "parallel"),
        name="dsa_prep",
    )(x3, norm_g.reshape(1, d), w_small, q_norm.reshape(1, Q_LORA), w_qb, kv_norm.reshape(1, KV_LORA),
      w_uk, w_iq, ik_g2, ik_b2)


def _dsa_attn_kernel(k_top, t_len, iq_ref, iw_ref, ik2_ref, qlat_ref, ckv_ref, gate_ref, wuv_ref, o_ref,
                     sc_ref, m_ref, l_ref, alpha_ref, p_ref, acc_ref):
    tq, tk = DSA_TQ, DSA_TK
    sub = SUBLANES
    i = pl.program_id(1)
    nkb = ((i + 1) * tq + tk - 1) // tk
    qpos8 = i * tq + lax.broadcasted_iota(I32, (sub, tq), 1)
    row8 = lax.broadcasted_iota(I32, (sub, tq), 0)

    def count(indicator):
        def body(j, cnt):
            k0 = pl.multiple_of(j * tk, tk)
            parts = [cnt, jnp.zeros_like(cnt), jnp.zeros_like(cnt), jnp.zeros_like(cnt)]
            for g in range(tk // sub):
                kb = sc_ref[pl.ds(k0 + g * sub, sub), :]
                parts[g % 4] = parts[g % 4] + indicator(kb, k0 + g * sub + row8)
            return (parts[0] + parts[1]) + (parts[2] + parts[3])

        cnt = lax.fori_loop(0, nkb, body, jnp.zeros((sub, tq), I32)).astype(F32)
        return jnp.broadcast_to(jnp.sum(cnt, axis=0, keepdims=True), (sub, tq))

    iq_all = iq_ref[...].reshape(IDX_HEADS // 2 * tq, LANES)
    iw_t = iw_ref[...].T
    iw8 = [jnp.broadcast_to(iw_t[h:h + 1, :], (sub, tq)) for h in range(IDX_HEADS)]

    def score_blk(j, _):
        k0 = pl.multiple_of(j * tk, tk)
        se = lax.dot_general(ik2_ref[pl.ds(k0, tk), 0:LANES], iq_all, NT_DIMS, preferred_element_type=F32)
        so = lax.dot_general(ik2_ref[pl.ds(k0, tk), LANES:2 * LANES], iq_all, NT_DIMS,
                             preferred_element_type=F32)
        for g in range(tk // sub):
            rows = slice(g * sub, (g + 1) * sub)
            sc = jnp.zeros((sub, tq), F32)
            for p in range(IDX_HEADS // 2):
                cols = slice(p * tq, (p + 1) * tq)
                sc = sc + jnp.maximum(se[rows, cols], 0.0) * iw8[2 * p]
                sc = sc + jnp.maximum(so[rows, cols], 0.0) * iw8[2 * p + 1]
            sc_ref[pl.ds(k0 + g * sub, sub), :] = jnp.where(k0 + g * sub + row8 <= qpos8, sc, -jnp.inf)
        return 0

    lax.fori_loop(0, nkb, score_blk, 0)

    def decode(code):
        bits = code ^ ((code >> 31) & 0x7FFFFFFF)
        return jnp.where(code < CODE_NEG_INF, -jnp.inf, pltpu.bitcast(bits, F32))

    def bit_body(b, ans):
        cand = ans + lax.shift_left(jnp.int32(1), 31 - b)
        cand_f = decode(cand)
        return jnp.where(count(lambda sb, row: jnp.where(sb >= cand_f, 1, 0)) >= k_top, cand, ans)

    thr = decode(lax.fori_loop(0, 32, bit_body, jnp.full((sub, tq), INT_MIN, I32)))

    n_ge = count(lambda kb, row: jnp.where(kb >= thr, 1, 0))
    need = k_top - count(lambda kb, row: jnp.where(kb > thr, 1, 0))
    excess = jnp.max(jnp.where(n_ge > k_top, 1.0, 0.0)) > 0.5
    idx_bits = (t_len - 1).bit_length()

    def tie_cut():
        def idx_body(b, d):
            cand = d + lax.shift_left(jnp.int32(1), idx_bits - 1 - b)
            below = count(lambda kb, row: jnp.where(kb == thr, jnp.where(row < cand, 1, 0), 0))
            return jnp.where(below < need, cand, d)

        d = lax.fori_loop(0, idx_bits, idx_body, jnp.zeros((sub, tq), I32))
        return jnp.where(n_ge > k_top, d, t_len)

    cut = lax.cond(excess, tie_cut, lambda: jnp.full((sub, tq), t_len, I32))

    nch = tk // LANES
    m_ref[...] = jnp.full_like(m_ref, NEG)
    l_ref[...] = jnp.zeros_like(l_ref)
    acc_ref[...] = jnp.zeros_like(acc_ref)
    thr_b = jnp.broadcast_to(thr[0:1, :], (LANES, tq))
    cut_b = jnp.broadcast_to(cut[0:1, :], (LANES, tq))
    qpos_b = i * tq + lax.broadcasted_iota(I32, (LANES, tq), 1)
    row_b = lax.broadcasted_iota(I32, (LANES, tq), 0)

    def attn_blk(j):
        k0 = pl.multiple_of(j * tk, tk)
        kv = ckv_ref[pl.ds(k0, tk), :]
        bias = []
        for c in range(nch):
            kb = sc_ref[pl.ds(k0 + c * LANES, LANES), :]
            kpos = k0 + c * LANES + row_b
            tie = jnp.where(kb == thr_b, jnp.where(kpos <= cut_b, 0.0, NEG), NEG)
            bias.append(jnp.where(kpos <= qpos_b, jnp.where(kb > thr_b, 0.0, tie), NEG).T)
        for g in range(DSA_HEADS // DSA_HEAD_GROUP):
            grows = slice(g * DSA_HEAD_GROUP * tq, (g + 1) * DSA_HEAD_GROUP * tq)
            s_g = lax.dot_general(qlat_ref[g * DSA_HEAD_GROUP:(g + 1) * DSA_HEAD_GROUP].reshape(
                DSA_HEAD_GROUP * tq, KV_LORA), kv, NT_DIMS, preferred_element_type=F32)
            for hh in range(DSA_HEAD_GROUP):
                rows = slice((g * DSA_HEAD_GROUP + hh) * tq, (g * DSA_HEAD_GROUP + hh + 1) * tq)
                s_chunks = [s_g[hh * tq:(hh + 1) * tq, c * LANES:(c + 1) * LANES] + bias[c] for c in range(nch)]
                m_new, alpha, l_new, p = _flash_update(s_chunks, m_ref[rows], l_ref[rows])
                m_ref[rows] = m_new
                l_ref[rows] = l_new
                alpha_ref[rows] = alpha
                p_ref[rows] = jnp.concatenate(p, axis=1).astype(BF16)
            alpha = alpha_ref[grows]
            acc_ref[grows] = jnp.concatenate([alpha] * (KV_LORA // LANES), axis=1) * acc_ref[grows] \
                + jnp.dot(p_ref[grows], kv, preferred_element_type=F32)

    def attn_pair(jj, _):
        attn_blk(2 * jj)
        attn_blk(2 * jj + 1)
        return 0

    lax.fori_loop(0, nkb // 2, attn_pair, 0)

    @pl.when(nkb % 2 == 1)
    def _():
        attn_blk(nkb - 1)

    for h in range(DSA_HEADS):
        rows = slice(h * tq, (h + 1) * tq)
        o_lat = (acc_ref[rows] / jnp.sum(l_ref[rows], axis=1, keepdims=True)).astype(BF16)
        o = jnp.dot(o_lat, wuv_ref[h], preferred_element_type=F32)
        cols = slice(h * DSA_HEAD_DIM, (h + 1) * DSA_HEAD_DIM)
        o_ref[:, cols] = (o * _silu(gate_ref[:, cols].astype(F32))).astype(o_ref.dtype)


def _dsa_attn(iq, iw, ik2, qlat, ckv, gate3, w_uv):
    b, t, _ = ckv.shape
    k_top = min(DSA_TOPK_MAX, t // 4)
    tq = DSA_TQ
    width = DSA_HEADS * DSA_HEAD_DIM
    return pl.pallas_call(
        functools.partial(_dsa_attn_kernel, k_top, t),
        grid=(b, t // tq),
        in_specs=[
            pl.BlockSpec((None, IDX_HEADS // 2, tq, LANES), lambda bi, i: (bi, 0, i, 0)),
            pl.BlockSpec((None, tq, LANES), lambda bi, i: (bi, i, 0)),
            pl.BlockSpec((None, t, 2 * LANES), lambda bi, i: (bi, 0, 0)),
            pl.BlockSpec((None, DSA_HEADS, tq, KV_LORA), lambda bi, i: (bi, 0, i, 0)),
            pl.BlockSpec((None, t, KV_LORA), lambda bi, i: (bi, 0, 0)),
            pl.BlockSpec((None, tq, width), lambda bi, i: (bi, i, 0)),
            pl.BlockSpec(w_uv.shape, lambda bi, i: (0, 0, 0)),
        ],
        out_specs=pl.BlockSpec((None, tq, width), lambda bi, i: (bi, i, 0)),
        out_shape=jax.ShapeDtypeStruct((b, t, width), BF16),
        scratch_shapes=[
            pltpu.VMEM((t, tq), F32),
            pltpu.VMEM((DSA_HEADS * tq, LANES), F32),
            pltpu.VMEM((DSA_HEADS * tq, LANES), F32),
            pltpu.VMEM((DSA_HEADS * tq, LANES), F32),
            pltpu.VMEM((DSA_HEADS * tq, DSA_TK), BF16),
            pltpu.VMEM((DSA_HEADS * tq, KV_LORA), F32),
        ],
        compiler_params=_params("parallel", "parallel"),
        name="dsa_attn",
    )(iq, iw, ik2, qlat, ckv, gate3, w_uv)


def _even_layer(x2, b, t, norm_g, w_in, b_glu, w_dw, b_dw, ln_g, ln_b, w_pw, b_pw, w_out, final_g, final):
    c = w_pw.shape[0]
    heads = (w_in.shape[1] - 3 * c) // (4 * MOBA_HEAD_DIM)
    z = _norm_matmul(x2, norm_g, w_in.astype(BF16), tm=1024, tn=1024)
    z3 = z.reshape(b, t, z.shape[1])
    ya = _conv_branch(z3, b_glu, w_dw, b_dw, ln_g, ln_b, w_pw.astype(BF16), b_pw, tt=256)
    yb = _moba_branch(z3, heads, 3 * c)
    w_out = w_out.astype(BF16)
    return _proj_residual(x2, [ya.reshape(b * t, c), yb.reshape(b * t, -1)], [w_out[:c], w_out[c:]],
                          final_g, final, tm=512)


def _odd_layer(x2, b, t, norm_g, w_in, q_norm, w_qb, kv_norm, w_uk, w_uv, w_iq, ik_g, ik_b, w_out,
               final_g, final):
    d = x2.shape[1]
    o_ik = Q_LORA + KV_LORA
    o_iw = o_ik + IDX_DIM
    o_gate = o_iw + IDX_HEADS
    zeros = lambda n: jnp.zeros((d, n), w_in.dtype)
    w_ik = w_in[:, o_ik:o_iw]
    w_small = jnp.concatenate(
        [w_in[:, :o_ik], w_ik, zeros(LANES - IDX_DIM), zeros(LANES - IDX_DIM), w_ik,
         w_in[:, o_iw:o_gate], zeros(LANES - IDX_HEADS)], axis=1).astype(BF16)
    pad = jnp.zeros((IDX_DIM,), F32)
    ik_g2 = jnp.concatenate([ik_g, pad, pad, ik_g]).reshape(1, 2 * LANES)
    ik_b2 = jnp.concatenate([ik_b, pad, pad, ik_b]).reshape(1, 2 * LANES)
    x3 = x2.reshape(b, t, d)
    qlat, iq, ckv, ik2, iw = _dsa_prep(x3, norm_g, w_small, q_norm, w_qb.astype(BF16), kv_norm,
                                       w_uk.astype(BF16), w_iq.astype(BF16), ik_g2, ik_b2, tm=256)
    gate = _norm_matmul(x2, norm_g, w_in[:, o_gate:].astype(BF16), tm=1024, tn=1024)
    og = _dsa_attn(iq, iw, ik2, qlat, ckv, gate.reshape(b, t, -1), w_uv.astype(BF16))
    return _proj_residual(x2, [og.reshape(b * t, -1)], [w_out.astype(BF16)], final_g, final, tm=512)


def kernel(x, even_norm, even_w_in, even_b_glu, even_w_dw, even_b_dw, even_conv_ln_g, even_conv_ln_b,
           even_w_pw, even_b_pw, even_w_out, odd_norm, odd_w_in, odd_q_norm, odd_w_qb, odd_kv_norm,
           odd_w_uk, odd_w_uv, odd_w_iq, odd_ik_ln_g, odd_ik_ln_b, odd_w_out, final_norm):
    b, t, d = x.shape
    depth = even_norm.shape[0] + odd_norm.shape[0]
    x2 = x.reshape(b * t, d)
    for layer in range(depth):
        i = layer // 2
        final = layer == depth - 1
        if layer % 2 == 0:
            x2 = _even_layer(x2, b, t, even_norm[i], even_w_in[i], even_b_glu[i], even_w_dw[i], even_b_dw[i],
                             even_conv_ln_g[i], even_conv_ln_b[i], even_w_pw[i], even_b_pw[i], even_w_out[i],
                             final_norm, final)
        else:
            x2 = _odd_layer(x2, b, t, odd_norm[i], odd_w_in[i], odd_q_norm[i], odd_w_qb[i], odd_kv_norm[i],
                            odd_w_uk[i], odd_w_uv[i], odd_w_iq[i], odd_ik_ln_g[i], odd_ik_ln_b[i],
                            odd_w_out[i], final_norm, final)
    return x2.reshape(b, t, d)
```

```python
import functools

import jax
import jax.numpy as jnp
from jax import lax
from jax.experimental import pallas as pl
from jax.experimental.pallas import tpu as pltpu

BF16 = jnp.bfloat16
F32 = jnp.float32
I32 = jnp.int32

EPS = 1e-6
LANES = 128
SUBLANES = 8
VMEM_LIMIT = 56 * 1024 * 1024

CONV_KERNEL = 31
CONV_HALO = 32
CONV_ROWS = 64
MOBA_HEAD_DIM = 128
MOBA_BLOCK = 256
MOBA_TOPK = 3
DSA_HEADS = 16
DSA_HEAD_DIM = 128
Q_LORA = 512
KV_LORA = 256
IDX_HEADS = 16
IDX_DIM = 64
DSA_TOPK_MAX = 256
DSA_TQ = 128
DSA_TK = 512
COUNT_SLAB = 64
DSA_HEAD_GROUP = 4
NEG = -1e30
LOG2E = 1.4426950408889634
INT_MIN = -2 ** 31
CODE_NEG_INF = (0xFF800000 ^ 0x7FFFFFFF) - 2 ** 32

NT_DIMS = (((1,), (1,)), ((), ()))


def _params(*sem):
    return pltpu.CompilerParams(dimension_semantics=sem, vmem_limit_bytes=VMEM_LIMIT)


def _rms(x, g):
    return x * lax.rsqrt(jnp.mean(x * x, axis=-1, keepdims=True) + EPS) * g


def _silu(x):
    return x * jax.nn.sigmoid(x)


def _norm_matmul_kernel(x_ref, g_ref, w_ref, o_ref, xn_ref):
    @pl.when(pl.program_id(1) == 0)
    def _():
        xn_ref[...] = _rms(x_ref[...], g_ref[...]).astype(BF16)

    o_ref[...] = jnp.dot(xn_ref[...], w_ref[...].astype(BF16), preferred_element_type=F32).astype(o_ref.dtype)


def _norm_matmul(x2, g, w_all, layer, tm, tn):
    n, d = x2.shape
    m = w_all.shape[2]
    return pl.pallas_call(
        _norm_matmul_kernel,
        grid=(n // tm, m // tn),
        in_specs=[
            pl.BlockSpec((tm, d), lambda i, j: (i, 0)),
            pl.BlockSpec((1, d), lambda i, j: (0, 0)),
            pl.BlockSpec((None, d, tn), lambda i, j: (layer, 0, j)),
        ],
        out_specs=pl.BlockSpec((tm, tn), lambda i, j: (i, j)),
        out_shape=jax.ShapeDtypeStruct((n, m), BF16),
        scratch_shapes=[pltpu.VMEM((tm, d), BF16)],
        compiler_params=_params("parallel", "arbitrary"),
        name="norm_matmul",
    )(x2, g.reshape(1, d), w_all)


def _proj_residual_kernel(n_in, final, *refs):
    x_ref = refs[0]
    a_refs = refs[1:1 + n_in]
    w_refs = refs[1 + n_in:1 + 2 * n_in]
    g_ref = refs[1 + 2 * n_in]
    o_ref = refs[2 + 2 * n_in]
    y = x_ref[...]
    for a_ref, w_ref in zip(a_refs, w_refs):
        y = y + jnp.dot(a_ref[...], w_ref[...], preferred_element_type=F32)
    if final:
        y = _rms(y, g_ref[...])
    o_ref[...] = y


def _proj_residual(x2, acts, w_all, layer, final_g, final, tm):
    n, d = x2.shape
    n_in = len(acts)
    width = acts[0].shape[1]
    assert all(a.shape[1] == width for a in acts) and n_in * width == w_all.shape[1]
    in_specs = [pl.BlockSpec((tm, d), lambda i: (i, 0))]
    in_specs += [pl.BlockSpec((tm, width), lambda i: (i, 0)) for _ in acts]
    in_specs += [pl.BlockSpec((None, width, d), functools.partial(lambda k, i: (layer, k, 0), k))
                 for k in range(n_in)]
    in_specs += [pl.BlockSpec((1, d), lambda i: (0, 0))]
    ws = [w_all] * n_in
    return pl.pallas_call(
        functools.partial(_proj_residual_kernel, n_in, final),
        grid=(n // tm,),
        in_specs=in_specs,
        out_specs=pl.BlockSpec((tm, d), lambda i: (i, 0)),
        out_shape=jax.ShapeDtypeStruct((n, d), F32),
        compiler_params=_params("parallel"),
        name="proj_residual",
    )(x2, *acts, *ws, final_g.reshape(1, d))


def _conv_kernel(tt, av_ref, ag_ref, gate_ref, hv_ref, hg_ref, bv_ref, bg_ref, wdw_ref, bdw_ref,
                 lng_ref, lnb_ref, wpw_ref, bpw_ref, o_ref, ubuf, shifted, ybuf):
    i = pl.program_id(1)
    bv = bv_ref[...]
    bg = bg_ref[...]

    def glu(v, g):
        return (v.astype(F32) + bv) * jax.nn.sigmoid(g.astype(F32) + bg)

    ubuf[0:CONV_HALO, :] = jnp.where(i > 0, glu(hv_ref[...], hg_ref[...]), 0.0)
    ubuf[CONV_HALO:, :] = glu(av_ref[...], ag_ref[...])
    span = tt + CONV_HALO - SUBLANES
    for b in range(1, SUBLANES):
        shifted[b - 1, 0:span, :] = ubuf[b:b + span, :]
    shift = CONV_HALO - (CONV_KERNEL - 1)
    c = ubuf.shape[1]
    for r0 in range(0, tt, CONV_ROWS):
        for c0 in range(0, c, LANES):
            acc = jnp.broadcast_to(bdw_ref[:, c0:c0 + LANES], (CONV_ROWS, LANES))
            for j in range(CONV_KERNEL):
                a, b = divmod(shift + j, SUBLANES)
                rows = slice(r0 + a * SUBLANES, r0 + a * SUBLANES + CONV_ROWS)
                src = ubuf[rows, c0:c0 + LANES] if b == 0 else shifted[b - 1, rows, c0:c0 + LANES]
                acc = acc + wdw_ref[j:j + 1, c0:c0 + LANES] * src
            ybuf[r0:r0 + CONV_ROWS, c0:c0 + LANES] = acc
    y = ybuf[...]
    mu = jnp.mean(y, axis=-1, keepdims=True)
    yc = y - mu
    var = jnp.mean(yc * yc, axis=-1, keepdims=True)
    y = _silu(yc * lax.rsqrt(var + EPS) * lng_ref[...] + lnb_ref[...])
    y = jnp.dot(y.astype(BF16), wpw_ref[...], preferred_element_type=F32) + bpw_ref[...]
    o_ref[...] = (y * _silu(gate_ref[...].astype(F32))).astype(o_ref.dtype)


def _conv_branch(z3, b_glu, w_dw, b_dw, ln_g, ln_b, w_pw_all, layer, b_pw, tt):
    b, t, _ = z3.shape
    c = w_pw_all.shape[1]
    hb = tt // CONV_HALO
    row = lambda v: v.reshape(1, c).astype(F32)
    w_dw_p = jnp.pad(w_dw.astype(F32), ((0, 32 - CONV_KERNEL), (0, 0)))
    const = lambda shape: pl.BlockSpec(shape, lambda bi, i: (0, 0))
    return pl.pallas_call(
        functools.partial(_conv_kernel, tt),
        grid=(b, t // tt),
        in_specs=[
            pl.BlockSpec((None, tt, c), lambda bi, i: (bi, i, 0)),
            pl.BlockSpec((None, tt, c), lambda bi, i: (bi, i, 1)),
            pl.BlockSpec((None, tt, c), lambda bi, i: (bi, i, 2)),
            pl.BlockSpec((None, CONV_HALO, c), lambda bi, i: (bi, jnp.maximum(i * hb - 1, 0), 0)),
            pl.BlockSpec((None, CONV_HALO, c), lambda bi, i: (bi, jnp.maximum(i * hb - 1, 0), 1)),
            const((1, c)), const((1, c)), const((32, c)), const((1, c)), const((1, c)), const((1, c)),
            pl.BlockSpec((None, c, c), lambda bi, i: (layer, 0, 0)), const((1, c)),
        ],
        out_specs=pl.BlockSpec((None, tt, c), lambda bi, i: (bi, i, 0)),
        out_shape=jax.ShapeDtypeStruct((b, t, c), BF16),
        scratch_shapes=[
            pltpu.VMEM((tt + CONV_HALO, c), F32),
            pltpu.VMEM((SUBLANES - 1, tt + CONV_HALO, c), F32),
            pltpu.VMEM((tt, c), F32),
        ],
        compiler_params=_params("parallel", "parallel"),
        name="conv_branch",
    )(z3, z3, z3, z3, z3, row(b_glu[:c]), row(b_glu[c:]), w_dw_p, row(b_dw), row(ln_g), row(ln_b),
      w_pw_all, row(b_pw))


def _flash_update(s_chunks, m_old, l_old):
    m_new = jnp.maximum(m_old, jnp.max(functools.reduce(jnp.maximum, s_chunks), axis=1, keepdims=True))
    alpha = jnp.exp2(m_old - m_new)
    p = [jnp.exp2(s - m_new) for s in s_chunks]
    return m_new, alpha, alpha * l_old + functools.reduce(jnp.add, p), p


def _moba_kernel(nb, heads, q_ref, k_ref, v_ref, g_ref, o_ref, kmean_ref, qa_ref, m_ref, l_ref, acc_ref):
    i = pl.program_id(1)
    blk, dh = MOBA_BLOCK, MOBA_HEAD_DIM
    nch = blk // LANES
    scale = dh ** -0.5 * LOG2E
    nbp = kmean_ref.shape[1]
    head = lambda h: slice(h * dh, (h + 1) * dh)

    @pl.when(i == 0)
    def _():
        kmean_ref[...] = jnp.zeros_like(kmean_ref)
        for n in range(nb):
            mean = jnp.mean(k_ref[n * blk:(n + 1) * blk, :].astype(F32), axis=0, keepdims=True)
            for h in range(heads):
                kmean_ref[h, n:n + 1, :] = mean[:, head(h)]

    n_iota = lax.broadcasted_iota(I32, (nbp, blk), 0)
    past = n_iota < i
    for h in range(heads):
        gate_t = lax.dot_general(kmean_ref[h].astype(BF16), q_ref[:, head(h)], NT_DIMS,
                                 preferred_element_type=F32)
        gate_t = jnp.where(past, gate_t, -jnp.inf)
        rank = jnp.zeros((nbp, blk), I32)
        for jp in range(nb - 1):
            row = gate_t[jp:jp + 1, :]
            rank = rank + jnp.where(row > gate_t, 1, jnp.where(row == gate_t, jnp.where(jp < n_iota, 1, 0), 0))
        pen_t = jnp.where(past, jnp.where(rank < MOBA_TOPK, 0.0, NEG), NEG)
        pen_t = jnp.concatenate([pen_t, jnp.zeros((LANES - nbp, blk), F32)], axis=0)
        qa_ref[h] = jnp.concatenate([q_ref[:, head(h)], pen_t.T.astype(BF16)], axis=1)

    m_ref[...] = jnp.full_like(m_ref, NEG)
    l_ref[...] = jnp.zeros_like(l_ref)
    acc_ref[...] = jnp.zeros_like(acc_ref)

    def step(h, j, s):
        k0 = pl.multiple_of(j * blk, blk)
        s_chunks = [s[:, c * LANES:(c + 1) * LANES] for c in range(nch)]
        m_new, alpha, l_new, p = _flash_update(s_chunks, m_ref[h], l_ref[h])
        m_ref[h] = m_new
        l_ref[h] = l_new
        pv = jnp.dot(jnp.concatenate(p, axis=1).astype(BF16), v_ref[pl.ds(k0, blk), head(h)],
                     preferred_element_type=F32)
        acc_ref[h] = alpha * acc_ref[h] + pv

    k_lane = lax.broadcasted_iota(I32, (blk, LANES), 1)

    def past_step(j):
        k0 = pl.multiple_of(j * blk, blk)
        onehot = jnp.where(k_lane == j, 1.0, 0.0).astype(BF16)
        for h in range(heads):
            ka = jnp.concatenate([k_ref[pl.ds(k0, blk), head(h)], onehot], axis=1)
            step(h, j, lax.dot_general(qa_ref[h], ka, NT_DIMS, preferred_element_type=F32) * scale)

    def past_pair(jj, _):
        past_step(2 * jj)
        past_step(2 * jj + 1)
        return 0

    lax.fori_loop(0, i // 2, past_pair, 0)

    @pl.when(i % 2 == 1)
    def _():
        past_step(i - 1)

    r = lax.broadcasted_iota(I32, (blk, blk), 0)
    c = lax.broadcasted_iota(I32, (blk, blk), 1)
    causal = jnp.where(c <= r, 0.0, NEG)
    k0 = pl.multiple_of(i * blk, blk)
    for h in range(heads):
        s = lax.dot_general(q_ref[:, head(h)], k_ref[pl.ds(k0, blk), head(h)], NT_DIMS,
                            preferred_element_type=F32) * scale
        step(h, i, s + causal)
        l = jnp.sum(l_ref[h], axis=1, keepdims=True)
        o_ref[:, head(h)] = (acc_ref[h] / l * _silu(g_ref[:, head(h)].astype(F32))).astype(o_ref.dtype)


def _moba_branch(z3, heads, col0):
    b, t, _ = z3.shape
    nb = t // MOBA_BLOCK
    w = heads * MOBA_HEAD_DIM
    c0 = col0 // w
    return pl.pallas_call(
        functools.partial(_moba_kernel, nb, heads),
        grid=(b, nb),
        in_specs=[
            pl.BlockSpec((None, MOBA_BLOCK, w), lambda bi, i: (bi, i, c0)),
            pl.BlockSpec((None, t, w), lambda bi, i: (bi, 0, c0 + 1)),
            pl.BlockSpec((None, t, w), lambda bi, i: (bi, 0, c0 + 2)),
            pl.BlockSpec((None, MOBA_BLOCK, w), lambda bi, i: (bi, i, c0 + 3)),
        ],
        out_specs=pl.BlockSpec((None, MOBA_BLOCK, w), lambda bi, i: (bi, i, 0)),
        out_shape=jax.ShapeDtypeStruct((b, t, w), BF16),
        scratch_shapes=[
            pltpu.VMEM((heads, -(-nb // SUBLANES) * SUBLANES, MOBA_HEAD_DIM), F32),
            pltpu.VMEM((heads, MOBA_BLOCK, MOBA_HEAD_DIM + LANES), BF16),
            pltpu.VMEM((heads, MOBA_BLOCK, LANES), F32),
            pltpu.VMEM((heads, MOBA_BLOCK, LANES), F32),
            pltpu.VMEM((heads, MOBA_BLOCK, MOBA_HEAD_DIM), F32),
        ],
        compiler_params=_params("parallel", "arbitrary"),
        name="moba_branch",
    )(z3, z3, z3, z3)


def _dsa_prep_kernel(x_ref, g_ref, ws_ref, qn_ref, wqb_ref, kvn_ref, wuk_ref, wiq_ref, ikg_ref, ikb_ref,
                     qlat_ref, iq_ref, ckv_ref, ik2_ref, iw_ref):
    scale = DSA_HEAD_DIM ** -0.5 * LOG2E
    xn = _rms(x_ref[...], g_ref[...]).astype(BF16)
    zs = jnp.dot(xn, ws_ref[...], preferred_element_type=F32)
    o_kv = Q_LORA
    o_ik = Q_LORA + KV_LORA
    o_iw = o_ik + 2 * LANES
    cqn = _rms(zs[:, :Q_LORA], qn_ref[...]).astype(BF16)
    q = jnp.dot(cqn, wqb_ref[...], preferred_element_type=F32).astype(BF16)
    for h in range(DSA_HEADS):
        ql = jnp.dot(q[:, h * DSA_HEAD_DIM:(h + 1) * DSA_HEAD_DIM], wuk_ref[h], preferred_element_type=F32)
        qlat_ref[h] = (ql * scale).astype(qlat_ref.dtype)
    iq = jnp.dot(cqn, wiq_ref[...], preferred_element_type=F32)
    for p in range(IDX_HEADS // 2):
        iq_ref[p] = iq[:, p * LANES:(p + 1) * LANES].astype(iq_ref.dtype)
    ckv_ref[...] = _rms(zs[:, o_kv:o_kv + KV_LORA], kvn_ref[...]).astype(ckv_ref.dtype)
    lane = lax.broadcasted_iota(I32, (1, LANES), 1)
    for half in range(2):
        z = zs[:, o_ik + half * LANES:o_ik + (half + 1) * LANES]
        valid = (lane < IDX_DIM) if half == 0 else (lane >= IDX_DIM)
        mu = jnp.sum(z, axis=-1, keepdims=True) * (1.0 / IDX_DIM)
        zc = jnp.where(valid, z - mu, 0.0)
        var = jnp.sum(zc * zc, axis=-1, keepdims=True) * (1.0 / IDX_DIM)
        y = zc * lax.rsqrt(var + EPS) * ikg_ref[:, half * LANES:(half + 1) * LANES] \
            + ikb_ref[:, half * LANES:(half + 1) * LANES]
        ik2_ref[:, half * LANES:(half + 1) * LANES] = y.astype(ik2_ref.dtype)
    iw_ref[...] = zs[:, o_iw:o_iw + LANES] * (IDX_HEADS ** -0.5 * IDX_DIM ** -0.5)


def _dsa_prep(x3, norm_g, w_small, q_norm, w_qb, kv_norm, w_uk, w_iq, layer, ik_g2, ik_b2, tm):
    b, t, d = x3.shape
    const = lambda shape: pl.BlockSpec(shape, lambda bi, i: (0,) * len(shape))
    stack = lambda w: pl.BlockSpec((None,) + w.shape[1:], lambda bi, i: (layer,) + (0,) * (w.ndim - 1))
    return pl.pallas_call(
        _dsa_prep_kernel,
        grid=(b, t // tm),
        in_specs=[
            pl.BlockSpec((None, tm, d), lambda bi, i: (bi, i, 0)),
            const((1, d)), stack(w_small), const((1, Q_LORA)), stack(w_qb),
            const((1, KV_LORA)), stack(w_uk), stack(w_iq), const((1, 2 * LANES)),
            const((1, 2 * LANES)),
        ],
        out_specs=[
            pl.BlockSpec((None, DSA_HEADS, tm, KV_LORA), lambda bi, i: (bi, 0, i, 0)),
            pl.BlockSpec((None, IDX_HEADS // 2, tm, LANES), lambda bi, i: (bi, 0, i, 0)),
            pl.BlockSpec((None, tm, KV_LORA), lambda bi, i: (bi, i, 0)),
            pl.BlockSpec((None, tm, 2 * LANES), lambda bi, i: (bi, i, 0)),
            pl.BlockSpec((None, tm, LANES), lambda bi, i: (bi, i, 0)),
        ],
        out_shape=[
            jax.ShapeDtypeStruct((b, DSA_HEADS, t, KV_LORA), BF16),
            jax.ShapeDtypeStruct((b, IDX_HEADS // 2, t, LANES), BF16),
            jax.ShapeDtypeStruct((b, t, KV_LORA), BF16),
            jax.ShapeDtypeStruct((b, t, 2 * LANES), BF16),
            jax.ShapeDtypeStruct((b, t, LANES), F32),
        ],
        compiler_params=_params("parallel", "parallel"),
        name="dsa_prep",
    )(x3, norm_g.reshape(1, d), w_small, q_norm.reshape(1, Q_LORA), w_qb, kv_norm.reshape(1, KV_LORA),
      w_uk, w_iq, ik_g2, ik_b2)


def _dsa_attn_kernel(k_top, t_len, iq_ref, iw_ref, ik2_ref, qlat_ref, ckv_ref, gate_ref, wuv_ref, o_ref,
                     sc_ref, m_ref, l_ref, alpha_ref, p_ref, acc_ref):
    tq, tk = DSA_TQ, DSA_TK
    sub = SUBLANES
    i = pl.program_id(1)
    nkb = ((i + 1) * tq + tk - 1) // tk
    qpos8 = i * tq + lax.broadcasted_iota(I32, (sub, tq), 1)
    row8 = lax.broadcasted_iota(I32, (sub, tq), 0)

    def count(indicator):
        def body(j, cnt):
            k0 = pl.multiple_of(j * tk, tk)
            parts = [cnt, jnp.zeros_like(cnt), jnp.zeros_like(cnt), jnp.zeros_like(cnt)]
            for s0 in range(0, tk, COUNT_SLAB):
                slab = sc_ref[pl.ds(k0 + s0, COUNT_SLAB), :]
                for g in range(COUNT_SLAB // sub):
                    kb = slab[g * sub:(g + 1) * sub, :]
                    parts[g % 4] = parts[g % 4] + indicator(kb, k0 + s0 + g * sub + row8)
            return (parts[0] + parts[1]) + (parts[2] + parts[3])

        cnt = lax.fori_loop(0, nkb, body, jnp.zeros((sub, tq), I32)).astype(F32)
        return jnp.broadcast_to(jnp.sum(cnt, axis=0, keepdims=True), (sub, tq))

    iq_all = iq_ref[...].reshape(IDX_HEADS // 2 * tq, LANES)
    iw_t = iw_ref[...].T
    iw8 = [jnp.broadcast_to(iw_t[h:h + 1, :], (sub, tq)) for h in range(IDX_HEADS)]

    def score_blk(j):
        k0 = pl.multiple_of(j * tk, tk)
        se = lax.dot_general(ik2_ref[pl.ds(k0, tk), 0:LANES], iq_all, NT_DIMS, preferred_element_type=F32)
        so = lax.dot_general(ik2_ref[pl.ds(k0, tk), LANES:2 * LANES], iq_all, NT_DIMS,
                             preferred_element_type=F32)
        for g in range(tk // sub):
            rows = slice(g * sub, (g + 1) * sub)
            sc = jnp.zeros((sub, tq), F32)
            for p in range(IDX_HEADS // 2):
                cols = slice(p * tq, (p + 1) * tq)
                sc = sc + jnp.maximum(se[rows, cols], 0.0) * iw8[2 * p]
                sc = sc + jnp.maximum(so[rows, cols], 0.0) * iw8[2 * p + 1]
            sc_ref[pl.ds(k0 + g * sub, sub), :] = jnp.where(k0 + g * sub + row8 <= qpos8, sc, -jnp.inf)

    def score_pair(jj, _):
        score_blk(2 * jj)
        score_blk(2 * jj + 1)
        return 0

    lax.fori_loop(0, nkb // 2, score_pair, 0)

    @pl.when(nkb % 2 == 1)
    def _():
        score_blk(nkb - 1)

    def decode(code):
        bits = code ^ ((code >> 31) & 0x7FFFFFFF)
        return jnp.where(code < CODE_NEG_INF, -jnp.inf, pltpu.bitcast(bits, F32))

    def bit_body(b, ans):
        cand = ans + lax.shift_left(jnp.int32(1), 31 - b)
        cand_f = decode(cand)
        return jnp.where(count(lambda sb, row: jnp.where(sb >= cand_f, 1, 0)) >= k_top, cand, ans)

    thr = decode(lax.fori_loop(0, 32, bit_body, jnp.full((sub, tq), INT_MIN, I32)))

    n_ge = count(lambda kb, row: jnp.where(kb >= thr, 1, 0))
    need = k_top - count(lambda kb, row: jnp.where(kb > thr, 1, 0))
    excess = jnp.max(jnp.where(n_ge > k_top, 1.0, 0.0)) > 0.5
    idx_bits = (t_len - 1).bit_length()

    def tie_cut():
        def idx_body(b, d):
            cand = d + lax.shift_left(jnp.int32(1), idx_bits - 1 - b)
            below = count(lambda kb, row: jnp.where(kb == thr, jnp.where(row < cand, 1, 0), 0))
            return jnp.where(below < need, cand, d)

        d = lax.fori_loop(0, idx_bits, idx_body, jnp.zeros((sub, tq), I32))
        return jnp.where(n_ge > k_top, d, t_len)

    cut = lax.cond(excess, tie_cut, lambda: jnp.full((sub, tq), t_len, I32))

    nch = tk // LANES
    m_ref[...] = jnp.full_like(m_ref, NEG)
    l_ref[...] = jnp.zeros_like(l_ref)
    acc_ref[...] = jnp.zeros_like(acc_ref)
    thr_b = jnp.broadcast_to(thr[0:1, :], (LANES, tq))
    cut_b = jnp.broadcast_to(cut[0:1, :], (LANES, tq))
    qpos_b = i * tq + lax.broadcasted_iota(I32, (LANES, tq), 1)
    row_b = lax.broadcasted_iota(I32, (LANES, tq), 0)

    def attn_blk(j):
        k0 = pl.multiple_of(j * tk, tk)
        kv = ckv_ref[pl.ds(k0, tk), :]
        bias = []
        for c in range(nch):
            kb = sc_ref[pl.ds(k0 + c * LANES, LANES), :]
            kpos = k0 + c * LANES + row_b
            tie = jnp.where(kb == thr_b, jnp.where(kpos <= cut_b, 0.0, NEG), NEG)
            bias.append(jnp.where(kpos <= qpos_b, jnp.where(kb > thr_b, 0.0, tie), NEG).T)
        for g in range(DSA_HEADS // DSA_HEAD_GROUP):
            grows = slice(g * DSA_HEAD_GROUP * tq, (g + 1) * DSA_HEAD_GROUP * tq)
            s_g = lax.dot_general(qlat_ref[g * DSA_HEAD_GROUP:(g + 1) * DSA_HEAD_GROUP].reshape(
                DSA_HEAD_GROUP * tq, KV_LORA), kv, NT_DIMS, preferred_element_type=F32)
            for hh in range(DSA_HEAD_GROUP):
                rows = slice((g * DSA_HEAD_GROUP + hh) * tq, (g * DSA_HEAD_GROUP + hh + 1) * tq)
                s_chunks = [s_g[hh * tq:(hh + 1) * tq, c * LANES:(c + 1) * LANES] + bias[c] for c in range(nch)]
                m_new, alpha, l_new, p = _flash_update(s_chunks, m_ref[rows], l_ref[rows])
                m_ref[rows] = m_new
                l_ref[rows] = l_new
                alpha_ref[rows] = alpha
                p_ref[rows] = jnp.concatenate(p, axis=1).astype(BF16)
            alpha = alpha_ref[grows]
            acc_ref[grows] = jnp.concatenate([alpha] * (KV_LORA // LANES), axis=1) * acc_ref[grows] \
                + jnp.dot(p_ref[grows], kv, preferred_element_type=F32)

    def attn_quad(jj, _):
        for u in range(4):
            attn_blk(4 * jj + u)
        return 0

    lax.fori_loop(0, nkb // 4, attn_quad, 0)

    @pl.when(nkb % 4 >= 2)
    def _():
        attn_blk(nkb // 4 * 4)
        attn_blk(nkb // 4 * 4 + 1)

    @pl.when(nkb % 2 == 1)
    def _():
        attn_blk(nkb - 1)

    for h in range(DSA_HEADS):
        rows = slice(h * tq, (h + 1) * tq)
        o_lat = (acc_ref[rows] / jnp.sum(l_ref[rows], axis=1, keepdims=True)).astype(BF16)
        o = jnp.dot(o_lat, wuv_ref[h], preferred_element_type=F32)
        cols = slice(h * DSA_HEAD_DIM, (h + 1) * DSA_HEAD_DIM)
        o_ref[:, cols] = (o * _silu(gate_ref[:, cols].astype(F32))).astype(o_ref.dtype)


def _dsa_attn(iq, iw, ik2, qlat, ckv, gate3, w_uv_all, layer):
    b, t, _ = ckv.shape
    k_top = min(DSA_TOPK_MAX, t // 4)
    tq = DSA_TQ
    width = DSA_HEADS * DSA_HEAD_DIM
    return pl.pallas_call(
        functools.partial(_dsa_attn_kernel, k_top, t),
        grid=(b, t // tq),
        in_specs=[
            pl.BlockSpec((None, IDX_HEADS // 2, tq, LANES), lambda bi, i: (bi, 0, i, 0)),
            pl.BlockSpec((None, tq, LANES), lambda bi, i: (bi, i, 0)),
            pl.BlockSpec((None, t, 2 * LANES), lambda bi, i: (bi, 0, 0)),
            pl.BlockSpec((None, DSA_HEADS, tq, KV_LORA), lambda bi, i: (bi, 0, i, 0)),
            pl.BlockSpec((None, t, KV_LORA), lambda bi, i: (bi, 0, 0)),
            pl.BlockSpec((None, tq, width), lambda bi, i: (bi, i, 0)),
            pl.BlockSpec((None,) + w_uv_all.shape[1:], lambda bi, i: (layer, 0, 0, 0)),
        ],
        out_specs=pl.BlockSpec((None, tq, width), lambda bi, i: (bi, i, 0)),
        out_shape=jax.ShapeDtypeStruct((b, t, width), BF16),
        scratch_shapes=[
            pltpu.VMEM((t, tq), F32),
            pltpu.VMEM((DSA_HEADS * tq, LANES), F32),
            pltpu.VMEM((DSA_HEADS * tq, LANES), F32),
            pltpu.VMEM((DSA_HEADS * tq, LANES), F32),
            pltpu.VMEM((DSA_HEADS * tq, DSA_TK), BF16),
            pltpu.VMEM((DSA_HEADS * tq, KV_LORA), F32),
        ],
        compiler_params=_params("parallel", "parallel"),
        name="dsa_attn",
    )(iq, iw, ik2, qlat, ckv, gate3, w_uv_all)


def _even_layer(x2, b, t, i, norm_g, w_in_all, b_glu, w_dw, b_dw, ln_g, ln_b, w_pw_all, b_pw, w_out_all,
                final_g, final):
    c = w_pw_all.shape[1]
    heads = (w_in_all.shape[2] - 3 * c) // (4 * MOBA_HEAD_DIM)
    z = _norm_matmul(x2, norm_g, w_in_all, i, tm=1024, tn=1024)
    z3 = z.reshape(b, t, z.shape[1])
    ya = _conv_branch(z3, b_glu, w_dw, b_dw, ln_g, ln_b, w_pw_all, i, b_pw, tt=256)
    yb = _moba_branch(z3, heads, 3 * c)
    return _proj_residual(x2, [ya.reshape(b * t, c), yb.reshape(b * t, -1)], w_out_all, i, final_g, final,
                          tm=512)


def _odd_weights(w_in, ik_g, ik_b):
    n_odd, d, _ = w_in.shape
    o_ik = Q_LORA + KV_LORA
    o_iw = o_ik + IDX_DIM
    o_gate = o_iw + IDX_HEADS
    zeros = lambda n: jnp.zeros((n_odd, d, n), w_in.dtype)
    w_ik = w_in[:, :, o_ik:o_iw]
    w_small = jnp.concatenate(
        [w_in[:, :, :o_ik], w_ik, zeros(LANES - IDX_DIM), zeros(LANES - IDX_DIM), w_ik,
         w_in[:, :, o_iw:o_gate], zeros(LANES - IDX_HEADS)], axis=2).astype(BF16)
    pad = jnp.zeros((n_odd, IDX_DIM), F32)
    ik_g2 = jnp.concatenate([ik_g, pad, pad, ik_g], axis=1)
    ik_b2 = jnp.concatenate([ik_b, pad, pad, ik_b], axis=1)
    return w_small, w_in[:, :, o_gate:].astype(BF16), ik_g2, ik_b2


def _odd_layer(x2, b, t, i, norm_g, w_small_all, w_gate_all, q_norm, w_qb_all, kv_norm, w_uk_all, w_uv_all,
               w_iq_all, ik_g2, ik_b2, w_out_all, final_g, final):
    d = x2.shape[1]
    x3 = x2.reshape(b, t, d)
    qlat, iq, ckv, ik2, iw = _dsa_prep(x3, norm_g, w_small_all, q_norm, w_qb_all, kv_norm, w_uk_all, w_iq_all,
                                       i, ik_g2.reshape(1, 2 * LANES), ik_b2.reshape(1, 2 * LANES), tm=256)
    gate = _norm_matmul(x2, norm_g, w_gate_all, i, tm=1024, tn=1024)
    og = _dsa_attn(iq, iw, ik2, qlat, ckv, gate.reshape(b, t, -1), w_uv_all, i)
    return _proj_residual(x2, [og.reshape(b * t, -1)], w_out_all, i, final_g, final, tm=512)


def kernel(x, even_norm, even_w_in, even_b_glu, even_w_dw, even_b_dw, even_conv_ln_g, even_conv_ln_b,
           even_w_pw, even_b_pw, even_w_out, odd_norm, odd_w_in, odd_q_norm, odd_w_qb, odd_kv_norm,
           odd_w_uk, odd_w_uv, odd_w_iq, odd_ik_ln_g, odd_ik_ln_b, odd_w_out, final_norm):
    b, t, d = x.shape
    depth = even_norm.shape[0] + odd_norm.shape[0]
    x2 = x.reshape(b * t, d)
    even_w_pw, even_w_out = even_w_pw.astype(BF16), even_w_out.astype(BF16)
    w_small, w_gate, ik_g2, ik_b2 = _odd_weights(odd_w_in, odd_ik_ln_g, odd_ik_ln_b)
    odd_w_qb, odd_w_uk, odd_w_uv = odd_w_qb.astype(BF16), odd_w_uk.astype(BF16), odd_w_uv.astype(BF16)
    odd_w_iq, odd_w_out = odd_w_iq.astype(BF16), odd_w_out.astype(BF16)
    for layer in range(depth):
        i = layer // 2
        final = layer == depth - 1
        if layer % 2 == 0:
            x2 = _even_layer(x2, b, t, i, even_norm[i], even_w_in, even_b_glu[i], even_w_dw[i], even_b_dw[i],
                             even_conv_ln_g[i], even_conv_ln_b[i], even_w_pw, even_b_pw[i], even_w_out,
                             final_norm, final)
        else:
            x2 = _odd_layer(x2, b, t, i, odd_norm[i], w_small, w_gate, odd_q_norm[i], odd_w_qb, odd_kv_norm[i],
                            odd_w_uk, odd_w_uv, odd_w_iq, ik_g2[i], ik_b2[i], odd_w_out, final_norm, final)
    return x2.reshape(b, t, d)
```

```python
import functools

import jax
import jax.numpy as jnp
from jax import lax
from jax.experimental import pallas as pl
from jax.experimental.pallas import tpu as pltpu

BF16 = jnp.bfloat16
F32 = jnp.float32
I32 = jnp.int32

EPS = 1e-6
LANES = 128
SUBLANES = 8
VMEM_LIMIT = 56 * 1024 * 1024

IN_PROJ_ROWS = 1024
IN_PROJ_COLS = 1024
OUT_PROJ_ROWS = 512
CONV_TILE_ROWS = 256
DSA_PREP_ROWS = 256

CONV_KERNEL = 31
CONV_HALO = 32
CONV_ROWS = 64
MOBA_HEAD_DIM = 128
MOBA_BLOCK = 256
MOBA_TOPK = 3
DSA_HEADS = 16
DSA_HEAD_DIM = 128
Q_LORA = 512
KV_LORA = 256
IDX_HEADS = 16
IDX_DIM = 64
DSA_TOPK_MAX = 256
DSA_TQ = 128
DSA_TK = 512
COUNT_SLAB = 64
DSA_HEAD_GROUP = 4
NEG = -1e30
LOG2E = 1.4426950408889634
INT_MIN = -2 ** 31
CODE_NEG_INF = (0xFF800000 ^ 0x7FFFFFFF) - 2 ** 32

NT_DIMS = (((1,), (1,)), ((), ()))


def _params(*sem):
    return pltpu.CompilerParams(dimension_semantics=sem, vmem_limit_bytes=VMEM_LIMIT)


def _rms(x, g):
    return x * lax.rsqrt(jnp.mean(x * x, axis=-1, keepdims=True) + EPS) * g


def _silu(x):
    return x * jax.nn.sigmoid(x)


def _norm_matmul_kernel(x_ref, g_ref, w_ref, o_ref, xn_ref):
    @pl.when(pl.program_id(1) == 0)
    def _():
        xn_ref[...] = _rms(x_ref[...], g_ref[...]).astype(BF16)

    o_ref[...] = jnp.dot(xn_ref[...], w_ref[...].astype(BF16), preferred_element_type=F32).astype(o_ref.dtype)


def _norm_matmul(x2, g, w_all, layer, tm, tn):
    n, d = x2.shape
    m = w_all.shape[2]
    return pl.pallas_call(
        _norm_matmul_kernel,
        grid=(n // tm, m // tn),
        in_specs=[
            pl.BlockSpec((tm, d), lambda i, j: (i, 0)),
            pl.BlockSpec((1, d), lambda i, j: (0, 0)),
            pl.BlockSpec((None, d, tn), lambda i, j: (layer, 0, j)),
        ],
        out_specs=pl.BlockSpec((tm, tn), lambda i, j: (i, j)),
        out_shape=jax.ShapeDtypeStruct((n, m), BF16),
        scratch_shapes=[pltpu.VMEM((tm, d), BF16)],
        compiler_params=_params("parallel", "arbitrary"),
        name="norm_matmul",
    )(x2, g.reshape(1, d), w_all)


def _proj_residual_kernel(n_in, final, *refs):
    x_ref = refs[0]
    a_refs = refs[1:1 + n_in]
    w_refs = refs[1 + n_in:1 + 2 * n_in]
    g_ref = refs[1 + 2 * n_in]
    o_ref = refs[2 + 2 * n_in]
    y = x_ref[...]
    for a_ref, w_ref in zip(a_refs, w_refs):
        y = y + jnp.dot(a_ref[...], w_ref[...], preferred_element_type=F32)
    if final:
        y = _rms(y, g_ref[...])
    o_ref[...] = y


def _proj_residual(x2, acts, w_all, layer, final_g, final, tm):
    n, d = x2.shape
    n_in = len(acts)
    width = acts[0].shape[1]
    assert all(a.shape[1] == width for a in acts) and n_in * width == w_all.shape[1]
    in_specs = [pl.BlockSpec((tm, d), lambda i: (i, 0))]
    in_specs += [pl.BlockSpec((tm, width), lambda i: (i, 0)) for _ in acts]
    in_specs += [pl.BlockSpec((None, width, d), functools.partial(lambda k, i: (layer, k, 0), k))
                 for k in range(n_in)]
    in_specs += [pl.BlockSpec((1, d), lambda i: (0, 0))]
    ws = [w_all] * n_in
    return pl.pallas_call(
        functools.partial(_proj_residual_kernel, n_in, final),
        grid=(n // tm,),
        in_specs=in_specs,
        out_specs=pl.BlockSpec((tm, d), lambda i: (i, 0)),
        out_shape=jax.ShapeDtypeStruct((n, d), F32),
        compiler_params=_params("parallel"),
        name="proj_residual",
    )(x2, *acts, *ws, final_g.reshape(1, d))


def _conv_kernel(tt, av_ref, ag_ref, gate_ref, hv_ref, hg_ref, bv_ref, bg_ref, wdw_ref, bdw_ref,
                 lng_ref, lnb_ref, wpw_ref, bpw_ref, o_ref, ubuf, shifted, ybuf):
    i = pl.program_id(1)
    bv = bv_ref[...]
    bg = bg_ref[...]

    def glu(v, g):
        return (v.astype(F32) + bv) * jax.nn.sigmoid(g.astype(F32) + bg)

    ubuf[0:CONV_HALO, :] = jnp.where(i > 0, glu(hv_ref[...], hg_ref[...]), 0.0)
    ubuf[CONV_HALO:, :] = glu(av_ref[...], ag_ref[...])
    span = tt + CONV_HALO - SUBLANES
    for b in range(1, SUBLANES):
        shifted[b - 1, 0:span, :] = ubuf[b:b + span, :]
    shift = CONV_HALO - (CONV_KERNEL - 1)
    c = ubuf.shape[1]
    for r0 in range(0, tt, CONV_ROWS):
        for c0 in range(0, c, LANES):
            acc = jnp.broadcast_to(bdw_ref[:, c0:c0 + LANES], (CONV_ROWS, LANES))
            for j in range(CONV_KERNEL):
                a, b = divmod(shift + j, SUBLANES)
                rows = slice(r0 + a * SUBLANES, r0 + a * SUBLANES + CONV_ROWS)
                src = ubuf[rows, c0:c0 + LANES] if b == 0 else shifted[b - 1, rows, c0:c0 + LANES]
                acc = acc + wdw_ref[j:j + 1, c0:c0 + LANES] * src
            ybuf[r0:r0 + CONV_ROWS, c0:c0 + LANES] = acc
    y = ybuf[...]
    mu = jnp.mean(y, axis=-1, keepdims=True)
    yc = y - mu
    var = jnp.mean(yc * yc, axis=-1, keepdims=True)
    y = _silu(yc * lax.rsqrt(var + EPS) * lng_ref[...] + lnb_ref[...])
    y = jnp.dot(y.astype(BF16), wpw_ref[...], preferred_element_type=F32) + bpw_ref[...]
    o_ref[...] = (y * _silu(gate_ref[...].astype(F32))).astype(o_ref.dtype)


def _conv_branch(z3, b_glu, w_dw, b_dw, ln_g, ln_b, w_pw_all, layer, b_pw, tt):
    b, t, _ = z3.shape
    c = w_pw_all.shape[1]
    hb = tt // CONV_HALO
    row = lambda v: v.reshape(1, c).astype(F32)
    w_dw_p = jnp.pad(w_dw.astype(F32), ((0, 32 - CONV_KERNEL), (0, 0)))
    const = lambda shape: pl.BlockSpec(shape, lambda bi, i: (0, 0))
    return pl.pallas_call(
        functools.partial(_conv_kernel, tt),
        grid=(b, t // tt),
        in_specs=[
            pl.BlockSpec((None, tt, c), lambda bi, i: (bi, i, 0)),
            pl.BlockSpec((None, tt, c), lambda bi, i: (bi, i, 1)),
            pl.BlockSpec((None, tt, c), lambda bi, i: (bi, i, 2)),
            pl.BlockSpec((None, CONV_HALO, c), lambda bi, i: (bi, jnp.maximum(i * hb - 1, 0), 0)),
            pl.BlockSpec((None, CONV_HALO, c), lambda bi, i: (bi, jnp.maximum(i * hb - 1, 0), 1)),
            const((1, c)), const((1, c)), const((32, c)), const((1, c)), const((1, c)), const((1, c)),
            pl.BlockSpec((None, c, c), lambda bi, i: (layer, 0, 0)), const((1, c)),
        ],
        out_specs=pl.BlockSpec((None, tt, c), lambda bi, i: (bi, i, 0)),
        out_shape=jax.ShapeDtypeStruct((b, t, c), BF16),
        scratch_shapes=[
            pltpu.VMEM((tt + CONV_HALO, c), F32),
            pltpu.VMEM((SUBLANES - 1, tt + CONV_HALO, c), F32),
            pltpu.VMEM((tt, c), F32),
        ],
        compiler_params=_params("parallel", "parallel"),
        name="conv_branch",
    )(z3, z3, z3, z3, z3, row(b_glu[:c]), row(b_glu[c:]), w_dw_p, row(b_dw), row(ln_g), row(ln_b),
      w_pw_all, row(b_pw))


def _flash_update(s_chunks, m_old, l_old):
    m_new = jnp.maximum(m_old, jnp.max(functools.reduce(jnp.maximum, s_chunks), axis=1, keepdims=True))
    alpha = jnp.exp2(m_old - m_new)
    p = [jnp.exp2(s - m_new) for s in s_chunks]
    return m_new, alpha, alpha * l_old + functools.reduce(jnp.add, p), p


def _moba_kernel(nb, heads, q_ref, k_ref, v_ref, g_ref, o_ref, kmean_ref, qa_ref, m_ref, l_ref, acc_ref):
    i = pl.program_id(1)
    blk, dh = MOBA_BLOCK, MOBA_HEAD_DIM
    nch = blk // LANES
    scale = dh ** -0.5 * LOG2E
    nbp = kmean_ref.shape[1]
    head = lambda h: slice(h * dh, (h + 1) * dh)

    @pl.when(i == 0)
    def _():
        kmean_ref[...] = jnp.zeros_like(kmean_ref)
        for n in range(nb):
            mean = jnp.mean(k_ref[n * blk:(n + 1) * blk, :].astype(F32), axis=0, keepdims=True)
            for h in range(heads):
                kmean_ref[h, n:n + 1, :] = mean[:, head(h)]

    n_iota = lax.broadcasted_iota(I32, (nbp, blk), 0)
    past = n_iota < i
    for h in range(heads):
        gate_t = lax.dot_general(kmean_ref[h].astype(BF16), q_ref[:, head(h)], NT_DIMS,
                                 preferred_element_type=F32)
        gate_t = jnp.where(past, gate_t, -jnp.inf)
        rank = jnp.zeros((nbp, blk), I32)
        for jp in range(nb - 1):
            row = gate_t[jp:jp + 1, :]
            rank = rank + jnp.where(row > gate_t, 1, jnp.where(row == gate_t, jnp.where(jp < n_iota, 1, 0), 0))
        pen_t = jnp.where(past, jnp.where(rank < MOBA_TOPK, 0.0, NEG), NEG)
        pen_t = jnp.concatenate([pen_t, jnp.zeros((LANES - nbp, blk), F32)], axis=0)
        qa_ref[h] = jnp.concatenate([q_ref[:, head(h)], pen_t.T.astype(BF16)], axis=1)

    m_ref[...] = jnp.full_like(m_ref, NEG)
    l_ref[...] = jnp.zeros_like(l_ref)
    acc_ref[...] = jnp.zeros_like(acc_ref)

    def step(h, j, s):
        k0 = pl.multiple_of(j * blk, blk)
        s_chunks = [s[:, c * LANES:(c + 1) * LANES] for c in range(nch)]
        m_new, alpha, l_new, p = _flash_update(s_chunks, m_ref[h], l_ref[h])
        m_ref[h] = m_new
        l_ref[h] = l_new
        pv = jnp.dot(jnp.concatenate(p, axis=1).astype(BF16), v_ref[pl.ds(k0, blk), head(h)],
                     preferred_element_type=F32)
        acc_ref[h] = alpha * acc_ref[h] + pv

    k_lane = lax.broadcasted_iota(I32, (blk, LANES), 1)

    def past_step(j):
        k0 = pl.multiple_of(j * blk, blk)
        onehot = jnp.where(k_lane == j, 1.0, 0.0).astype(BF16)
        for h in range(heads):
            ka = jnp.concatenate([k_ref[pl.ds(k0, blk), head(h)], onehot], axis=1)
            step(h, j, lax.dot_general(qa_ref[h], ka, NT_DIMS, preferred_element_type=F32) * scale)

    def past_quad(jj, _):
        for u in range(4):
            past_step(4 * jj + u)
        return 0

    lax.fori_loop(0, i // 4, past_quad, 0)

    @pl.when(i % 4 >= 2)
    def _():
        past_step(i // 4 * 4)
        past_step(i // 4 * 4 + 1)

    @pl.when(i % 2 == 1)
    def _():
        past_step(i - 1)

    r = lax.broadcasted_iota(I32, (blk, blk), 0)
    c = lax.broadcasted_iota(I32, (blk, blk), 1)
    causal = jnp.where(c <= r, 0.0, NEG)
    k0 = pl.multiple_of(i * blk, blk)
    for h in range(heads):
        s = lax.dot_general(q_ref[:, head(h)], k_ref[pl.ds(k0, blk), head(h)], NT_DIMS,
                            preferred_element_type=F32) * scale
        step(h, i, s + causal)
        l = jnp.sum(l_ref[h], axis=1, keepdims=True)
        o_ref[:, head(h)] = (acc_ref[h] / l * _silu(g_ref[:, head(h)].astype(F32))).astype(o_ref.dtype)


def _moba_branch(z3, heads, col0):
    b, t, _ = z3.shape
    nb = t // MOBA_BLOCK
    w = heads * MOBA_HEAD_DIM
    c0 = col0 // w
    return pl.pallas_call(
        functools.partial(_moba_kernel, nb, heads),
        grid=(b, nb),
        in_specs=[
            pl.BlockSpec((None, MOBA_BLOCK, w), lambda bi, i: (bi, i, c0)),
            pl.BlockSpec((None, t, w), lambda bi, i: (bi, 0, c0 + 1)),
            pl.BlockSpec((None, t, w), lambda bi, i: (bi, 0, c0 + 2)),
            pl.BlockSpec((None, MOBA_BLOCK, w), lambda bi, i: (bi, i, c0 + 3)),
        ],
        out_specs=pl.BlockSpec((None, MOBA_BLOCK, w), lambda bi, i: (bi, i, 0)),
        out_shape=jax.ShapeDtypeStruct((b, t, w), BF16),
        scratch_shapes=[
            pltpu.VMEM((heads, -(-nb // SUBLANES) * SUBLANES, MOBA_HEAD_DIM), F32),
            pltpu.VMEM((heads, MOBA_BLOCK, MOBA_HEAD_DIM + LANES), BF16),
            pltpu.VMEM((heads, MOBA_BLOCK, LANES), F32),
            pltpu.VMEM((heads, MOBA_BLOCK, LANES), F32),
            pltpu.VMEM((heads, MOBA_BLOCK, MOBA_HEAD_DIM), F32),
        ],
        compiler_params=_params("parallel", "arbitrary"),
        name="moba_branch",
    )(z3, z3, z3, z3)


def _dsa_prep_kernel(x_ref, g_ref, ws_ref, qn_ref, wqb_ref, kvn_ref, wuk_ref, wiq_ref, ikg_ref, ikb_ref,
                     qlat_ref, iq_ref, ckv_ref, ik2_ref, iw_ref):
    scale = DSA_HEAD_DIM ** -0.5 * LOG2E
    xn = _rms(x_ref[...], g_ref[...]).astype(BF16)
    zs = jnp.dot(xn, ws_ref[...], preferred_element_type=F32)
    o_kv = Q_LORA
    o_ik = Q_LORA + KV_LORA
    o_iw = o_ik + 2 * LANES
    cqn = _rms(zs[:, :Q_LORA], qn_ref[...]).astype(BF16)
    q = jnp.dot(cqn, wqb_ref[...], preferred_element_type=F32).astype(BF16)
    for h in range(DSA_HEADS):
        ql = jnp.dot(q[:, h * DSA_HEAD_DIM:(h + 1) * DSA_HEAD_DIM], wuk_ref[h], preferred_element_type=F32)
        qlat_ref[h] = (ql * scale).astype(qlat_ref.dtype)
    iq = jnp.dot(cqn, wiq_ref[...], preferred_element_type=F32)
    for p in range(IDX_HEADS // 2):
        iq_ref[p] = iq[:, p * LANES:(p + 1) * LANES].astype(iq_ref.dtype)
    ckv_ref[...] = _rms(zs[:, o_kv:o_kv + KV_LORA], kvn_ref[...]).astype(ckv_ref.dtype)
    lane = lax.broadcasted_iota(I32, (1, LANES), 1)
    for half in range(2):
        z = zs[:, o_ik + half * LANES:o_ik + (half + 1) * LANES]
        valid = (lane < IDX_DIM) if half == 0 else (lane >= IDX_DIM)
        mu = jnp.sum(z, axis=-1, keepdims=True) * (1.0 / IDX_DIM)
        zc = jnp.where(valid, z - mu, 0.0)
        var = jnp.sum(zc * zc, axis=-1, keepdims=True) * (1.0 / IDX_DIM)
        y = zc * lax.rsqrt(var + EPS) * ikg_ref[:, half * LANES:(half + 1) * LANES] \
            + ikb_ref[:, half * LANES:(half + 1) * LANES]
        ik2_ref[:, half * LANES:(half + 1) * LANES] = y.astype(ik2_ref.dtype)
    iw_ref[...] = zs[:, o_iw:o_iw + LANES] * (IDX_HEADS ** -0.5 * IDX_DIM ** -0.5)


def _dsa_prep(x3, norm_g, w_small, q_norm, w_qb, kv_norm, w_uk, w_iq, layer, ik_g2, ik_b2, tm):
    b, t, d = x3.shape
    const = lambda shape: pl.BlockSpec(shape, lambda bi, i: (0,) * len(shape))
    stack = lambda w: pl.BlockSpec((None,) + w.shape[1:], lambda bi, i: (layer,) + (0,) * (w.ndim - 1))
    return pl.pallas_call(
        _dsa_prep_kernel,
        grid=(b, t // tm),
        in_specs=[
            pl.BlockSpec((None, tm, d), lambda bi, i: (bi, i, 0)),
            const((1, d)), stack(w_small), const((1, Q_LORA)), stack(w_qb),
            const((1, KV_LORA)), stack(w_uk), stack(w_iq), const((1, 2 * LANES)),
            const((1, 2 * LANES)),
        ],
        out_specs=[
            pl.BlockSpec((None, DSA_HEADS, tm, KV_LORA), lambda bi, i: (bi, 0, i, 0)),
            pl.BlockSpec((None, IDX_HEADS // 2, tm, LANES), lambda bi, i: (bi, 0, i, 0)),
            pl.BlockSpec((None, tm, KV_LORA), lambda bi, i: (bi, i, 0)),
            pl.BlockSpec((None, tm, 2 * LANES), lambda bi, i: (bi, i, 0)),
            pl.BlockSpec((None, tm, LANES), lambda bi, i: (bi, i, 0)),
        ],
        out_shape=[
            jax.ShapeDtypeStruct((b, DSA_HEADS, t, KV_LORA), BF16),
            jax.ShapeDtypeStruct((b, IDX_HEADS // 2, t, LANES), BF16),
            jax.ShapeDtypeStruct((b, t, KV_LORA), BF16),
            jax.ShapeDtypeStruct((b, t, 2 * LANES), BF16),
            jax.ShapeDtypeStruct((b, t, LANES), F32),
        ],
        compiler_params=_params("parallel", "parallel"),
        name="dsa_prep",
    )(x3, norm_g.reshape(1, d), w_small, q_norm.reshape(1, Q_LORA), w_qb, kv_norm.reshape(1, KV_LORA),
      w_uk, w_iq, ik_g2, ik_b2)


def _dsa_attn_kernel(k_top, t_len, iq_ref, iw_ref, ik2_ref, qlat_ref, ckv_ref, gate_ref, wuv_ref, o_ref,
                     sc_ref, m_ref, l_ref, alpha_ref, p_ref, acc_ref):
    tq, tk = DSA_TQ, DSA_TK
    sub = SUBLANES
    i = pl.program_id(1)
    nkb = ((i + 1) * tq + tk - 1) // tk
    qpos8 = i * tq + lax.broadcasted_iota(I32, (sub, tq), 1)
    row8 = lax.broadcasted_iota(I32, (sub, tq), 0)

    def count(indicator):
        def body(j, cnt):
            k0 = pl.multiple_of(j * tk, tk)
            parts = [cnt, jnp.zeros_like(cnt), jnp.zeros_like(cnt), jnp.zeros_like(cnt)]
            for s0 in range(0, tk, COUNT_SLAB):
                slab = sc_ref[pl.ds(k0 + s0, COUNT_SLAB), :]
                for g in range(COUNT_SLAB // sub):
                    kb = slab[g * sub:(g + 1) * sub, :]
                    parts[g % 4] = parts[g % 4] + indicator(kb, k0 + s0 + g * sub + row8)
            return (parts[0] + parts[1]) + (parts[2] + parts[3])

        cnt = lax.fori_loop(0, nkb, body, jnp.zeros((sub, tq), I32)).astype(F32)
        return jnp.broadcast_to(jnp.sum(cnt, axis=0, keepdims=True), (sub, tq))

    iq_all = iq_ref[...].reshape(IDX_HEADS // 2 * tq, LANES)
    iw_t = iw_ref[...].T
    iw8 = [jnp.broadcast_to(iw_t[h:h + 1, :], (sub, tq)) for h in range(IDX_HEADS)]

    def score_blk(j):
        k0 = pl.multiple_of(j * tk, tk)
        se = lax.dot_general(ik2_ref[pl.ds(k0, tk), 0:LANES], iq_all, NT_DIMS, preferred_element_type=F32)
        so = lax.dot_general(ik2_ref[pl.ds(k0, tk), LANES:2 * LANES], iq_all, NT_DIMS,
                             preferred_element_type=F32)
        for g in range(tk // sub):
            rows = slice(g * sub, (g + 1) * sub)
            sc = jnp.zeros((sub, tq), F32)
            for p in range(IDX_HEADS // 2):
                cols = slice(p * tq, (p + 1) * tq)
                sc = sc + jnp.maximum(se[rows, cols], 0.0) * iw8[2 * p]
                sc = sc + jnp.maximum(so[rows, cols], 0.0) * iw8[2 * p + 1]
            sc_ref[pl.ds(k0 + g * sub, sub), :] = jnp.where(k0 + g * sub + row8 <= qpos8, sc, -jnp.inf)

    def score_pair(jj, _):
        score_blk(2 * jj)
        score_blk(2 * jj + 1)
        return 0

    lax.fori_loop(0, nkb // 2, score_pair, 0)

    @pl.when(nkb % 2 == 1)
    def _():
        score_blk(nkb - 1)

    def decode(code):
        bits = code ^ ((code >> 31) & 0x7FFFFFFF)
        return jnp.where(code < CODE_NEG_INF, -jnp.inf, pltpu.bitcast(bits, F32))

    def bit_body(b, carry):
        ans, n_ans = carry
        cand = ans + lax.shift_left(jnp.int32(1), 31 - b)
        cand_f = decode(cand)
        n = count(lambda sb, row: jnp.where(sb >= cand_f, 1, 0))
        return jnp.where(n >= k_top, cand, ans), jnp.where(n >= k_top, n, n_ans)

    scored = (nkb * tk).astype(F32)
    thr_code, n_ge = lax.fori_loop(
        0, 32, bit_body, (jnp.full((sub, tq), INT_MIN, I32), jnp.broadcast_to(scored, (sub, tq))))
    thr = decode(thr_code)

    excess = jnp.max(jnp.where(n_ge > k_top, 1.0, 0.0)) > 0.5
    idx_bits = (t_len - 1).bit_length()

    def tie_cut():
        need = k_top - count(lambda kb, row: jnp.where(kb > thr, 1, 0))

        def idx_body(b, d):
            cand = d + lax.shift_left(jnp.int32(1), idx_bits - 1 - b)
            below = count(lambda kb, row: jnp.where(kb == thr, jnp.where(row < cand, 1, 0), 0))
            return jnp.where(below < need, cand, d)

        d = lax.fori_loop(0, idx_bits, idx_body, jnp.zeros((sub, tq), I32))
        return jnp.where(n_ge > k_top, d, t_len)

    cut = lax.cond(excess, tie_cut, lambda: jnp.full((sub, tq), t_len, I32))

    nch = tk // LANES
    m_ref[...] = jnp.full_like(m_ref, NEG)
    l_ref[...] = jnp.zeros_like(l_ref)
    acc_ref[...] = jnp.zeros_like(acc_ref)
    thr_b = jnp.broadcast_to(thr[0:1, :], (LANES, tq))
    cut_b = jnp.broadcast_to(cut[0:1, :], (LANES, tq))
    qpos_b = i * tq + lax.broadcasted_iota(I32, (LANES, tq), 1)
    row_b = lax.broadcasted_iota(I32, (LANES, tq), 0)

    def attn_blk(j):
        k0 = pl.multiple_of(j * tk, tk)
        kv = ckv_ref[pl.ds(k0, tk), :]
        bias = []
        for c in range(nch):
            kb = sc_ref[pl.ds(k0 + c * LANES, LANES), :]
            kpos = k0 + c * LANES + row_b
            tie = jnp.where(kb == thr_b, jnp.where(kpos <= cut_b, 0.0, NEG), NEG)
            bias.append(jnp.where(kpos <= qpos_b, jnp.where(kb > thr_b, 0.0, tie), NEG).T)
        for g in range(DSA_HEADS // DSA_HEAD_GROUP):
            grows = slice(g * DSA_HEAD_GROUP * tq, (g + 1) * DSA_HEAD_GROUP * tq)
            s_g = lax.dot_general(qlat_ref[g * DSA_HEAD_GROUP:(g + 1) * DSA_HEAD_GROUP].reshape(
                DSA_HEAD_GROUP * tq, KV_LORA), kv, NT_DIMS, preferred_element_type=F32)
            for hh in range(DSA_HEAD_GROUP):
                rows = slice((g * DSA_HEAD_GROUP + hh) * tq, (g * DSA_HEAD_GROUP + hh + 1) * tq)
                s_chunks = [s_g[hh * tq:(hh + 1) * tq, c * LANES:(c + 1) * LANES] + bias[c] for c in range(nch)]
                m_new, alpha, l_new, p = _flash_update(s_chunks, m_ref[rows], l_ref[rows])
                m_ref[rows] = m_new
                l_ref[rows] = l_new
                alpha_ref[rows] = alpha
                p_ref[rows] = jnp.concatenate(p, axis=1).astype(BF16)
            alpha = alpha_ref[grows]
            acc_ref[grows] = jnp.concatenate([alpha] * (KV_LORA // LANES), axis=1) * acc_ref[grows] \
                + jnp.dot(p_ref[grows], kv, preferred_element_type=F32)

    def attn_quad(jj, _):
        for u in range(4):
            attn_blk(4 * jj + u)
        return 0

    lax.fori_loop(0, nkb // 4, attn_quad, 0)

    @pl.when(nkb % 4 >= 2)
    def _():
        attn_blk(nkb // 4 * 4)
        attn_blk(nkb // 4 * 4 + 1)

    @pl.when(nkb % 2 == 1)
    def _():
        attn_blk(nkb - 1)

    for h in range(DSA_HEADS):
        rows = slice(h * tq, (h + 1) * tq)
        o_lat = (acc_ref[rows] / jnp.sum(l_ref[rows], axis=1, keepdims=True)).astype(BF16)
        o = jnp.dot(o_lat, wuv_ref[h], preferred_element_type=F32)
        cols = slice(h * DSA_HEAD_DIM, (h + 1) * DSA_HEAD_DIM)
        o_ref[:, cols] = (o * _silu(gate_ref[:, cols].astype(F32))).astype(o_ref.dtype)


def _dsa_attn(iq, iw, ik2, qlat, ckv, gate3, w_uv_all, layer):
    b, t, _ = ckv.shape
    k_top = min(DSA_TOPK_MAX, t // 4)
    tq = DSA_TQ
    width = DSA_HEADS * DSA_HEAD_DIM
    return pl.pallas_call(
        functools.partial(_dsa_attn_kernel, k_top, t),
        grid=(b, t // tq),
        in_specs=[
            pl.BlockSpec((None, IDX_HEADS // 2, tq, LANES), lambda bi, i: (bi, 0, i, 0)),
            pl.BlockSpec((None, tq, LANES), lambda bi, i: (bi, i, 0)),
            pl.BlockSpec((None, t, 2 * LANES), lambda bi, i: (bi, 0, 0)),
            pl.BlockSpec((None, DSA_HEADS, tq, KV_LORA), lambda bi, i: (bi, 0, i, 0)),
            pl.BlockSpec((None, t, KV_LORA), lambda bi, i: (bi, 0, 0)),
            pl.BlockSpec((None, tq, width), lambda bi, i: (bi, i, 0)),
            pl.BlockSpec((None,) + w_uv_all.shape[1:], lambda bi, i: (layer, 0, 0, 0)),
        ],
        out_specs=pl.BlockSpec((None, tq, width), lambda bi, i: (bi, i, 0)),
        out_shape=jax.ShapeDtypeStruct((b, t, width), BF16),
        scratch_shapes=[
            pltpu.VMEM((t, tq), F32),
            pltpu.VMEM((DSA_HEADS * tq, LANES), F32),
            pltpu.VMEM((DSA_HEADS * tq, LANES), F32),
            pltpu.VMEM((DSA_HEADS * tq, LANES), F32),
            pltpu.VMEM((DSA_HEADS * tq, DSA_TK), BF16),
            pltpu.VMEM((DSA_HEADS * tq, KV_LORA), F32),
        ],
        compiler_params=_params("parallel", "parallel"),
        name="dsa_attn",
    )(iq, iw, ik2, qlat, ckv, gate3, w_uv_all)


def _even_layer(x2, b, t, i, norm_g, w_in_all, b_glu, w_dw, b_dw, ln_g, ln_b, w_pw_all, b_pw, w_out_all,
                final_g, final):
    c = w_pw_all.shape[1]
    heads = (w_in_all.shape[2] - 3 * c) // (4 * MOBA_HEAD_DIM)
    z = _norm_matmul(x2, norm_g, w_in_all, i, tm=IN_PROJ_ROWS, tn=IN_PROJ_COLS)
    z3 = z.reshape(b, t, z.shape[1])
    ya = _conv_branch(z3, b_glu, w_dw, b_dw, ln_g, ln_b, w_pw_all, i, b_pw, tt=CONV_TILE_ROWS)
    yb = _moba_branch(z3, heads, 3 * c)
    return _proj_residual(x2, [ya.reshape(b * t, c), yb.reshape(b * t, -1)], w_out_all, i, final_g, final,
                          tm=OUT_PROJ_ROWS)


def _odd_weights(w_in, ik_g, ik_b):
    n_odd, d, _ = w_in.shape
    o_ik = Q_LORA + KV_LORA
    o_iw = o_ik + IDX_DIM
    o_gate = o_iw + IDX_HEADS
    zeros = lambda n: jnp.zeros((n_odd, d, n), w_in.dtype)
    w_ik = w_in[:, :, o_ik:o_iw]
    w_small = jnp.concatenate(
        [w_in[:, :, :o_ik], w_ik, zeros(LANES - IDX_DIM), zeros(LANES - IDX_DIM), w_ik,
         w_in[:, :, o_iw:o_gate], zeros(LANES - IDX_HEADS)], axis=2).astype(BF16)
    pad = jnp.zeros((n_odd, IDX_DIM), F32)
    ik_g2 = jnp.concatenate([ik_g, pad, pad, ik_g], axis=1)
    ik_b2 = jnp.concatenate([ik_b, pad, pad, ik_b], axis=1)
    return w_small, w_in[:, :, o_gate:].astype(BF16), ik_g2, ik_b2


def _odd_layer(x2, b, t, i, norm_g, w_small_all, w_gate_all, q_norm, w_qb_all, kv_norm, w_uk_all, w_uv_all,
               w_iq_all, ik_g2, ik_b2, w_out_all, final_g, final):
    d = x2.shape[1]
    x3 = x2.reshape(b, t, d)
    qlat, iq, ckv, ik2, iw = _dsa_prep(x3, norm_g, w_small_all, q_norm, w_qb_all, kv_norm, w_uk_all, w_iq_all,
                                       i, ik_g2.reshape(1, 2 * LANES), ik_b2.reshape(1, 2 * LANES), tm=DSA_PREP_ROWS)
    gate = _norm_matmul(x2, norm_g, w_gate_all, i, tm=IN_PROJ_ROWS, tn=IN_PROJ_COLS)
    og = _dsa_attn(iq, iw, ik2, qlat, ckv, gate.reshape(b, t, -1), w_uv_all, i)
    return _proj_residual(x2, [og.reshape(b * t, -1)], w_out_all, i, final_g, final, tm=OUT_PROJ_ROWS)


def kernel(x, even_norm, even_w_in, even_b_glu, even_w_dw, even_b_dw, even_conv_ln_g, even_conv_ln_b,
           even_w_pw, even_b_pw, even_w_out, odd_norm, odd_w_in, odd_q_norm, odd_w_qb, odd_kv_norm,
           odd_w_uk, odd_w_uv, odd_w_iq, odd_ik_ln_g, odd_ik_ln_b, odd_w_out, final_norm):
    b, t, d = x.shape
    depth = even_norm.shape[0] + odd_norm.shape[0]
    x2 = x.reshape(b * t, d)
    even_w_pw, even_w_out = even_w_pw.astype(BF16), even_w_out.astype(BF16)
    w_small, w_gate, ik_g2, ik_b2 = _odd_weights(odd_w_in, odd_ik_ln_g, odd_ik_ln_b)
    odd_w_qb, odd_w_uk, odd_w_uv = odd_w_qb.astype(BF16), odd_w_uk.astype(BF16), odd_w_uv.astype(BF16)
    odd_w_iq, odd_w_out = odd_w_iq.astype(BF16), odd_w_out.astype(BF16)
    for layer in range(depth):
        i = layer // 2
        final = layer == depth - 1
        if layer % 2 == 0:
            x2 = _even_layer(x2, b, t, i, even_norm[i], even_w_in, even_b_glu[i], even_w_dw[i], even_b_dw[i],
                             even_conv_ln_g[i], even_conv_ln_b[i], even_w_pw, even_b_pw[i], even_w_out,
                             final_norm, final)
        else:
            x2 = _odd_layer(x2, b, t, i, odd_norm[i], w_small, w_gate, odd_q_norm[i], odd_w_qb, odd_kv_norm[i],
                            odd_w_uk, odd_w_uv, odd_w_iq, ik_g2[i], ik_b2[i], odd_w_out, final_norm, final)
    return x2.reshape(b, t, d)
```

```python
import functools

import jax
import jax.numpy as jnp
from jax import lax
from jax.experimental import pallas as pl
from jax.experimental.pallas import tpu as pltpu

BF16 = jnp.bfloat16
F32 = jnp.float32
I32 = jnp.int32

EPS = 1e-6
LANES = 128
SUBLANES = 8
VMEM_LIMIT = 56 * 1024 * 1024

IN_PROJ_ROWS = 1024
IN_PROJ_COLS = 1024
OUT_PROJ_ROWS = 512
CONV_TILE_ROWS = 256
DSA_PREP_ROWS = 256

CONV_KERNEL = 31
CONV_HALO = 32
CONV_ROWS = 64
MOBA_HEAD_DIM = 128
MOBA_BLOCK = 256
MOBA_TOPK = 3
DSA_HEADS = 16
DSA_HEAD_DIM = 128
Q_LORA = 512
KV_LORA = 256
IDX_HEADS = 16
IDX_DIM = 64
DSA_TOPK_MAX = 256
DSA_SMALL_COLS = -(-(Q_LORA + KV_LORA + IDX_DIM + IDX_HEADS) // LANES) * LANES
DSA_TQ = 128
DSA_TK = 512
DSA_ATTN_TK = 512
COUNT_SLAB = 64
DSA_HEAD_GROUP = 4
NEG = -1e30
LOG2E = 1.4426950408889634
INT_MIN = -2 ** 31
CODE_NEG_INF = (0xFF800000 ^ 0x7FFFFFFF) - 2 ** 32

NT_DIMS = (((1,), (1,)), ((), ()))


def _params(*sem):
    return pltpu.CompilerParams(dimension_semantics=sem, vmem_limit_bytes=VMEM_LIMIT)


def _rms(x, g):
    return x * lax.rsqrt(jnp.mean(x * x, axis=-1, keepdims=True) + EPS) * g


def _silu(x):
    return x * jax.nn.sigmoid(x)


def _norm_matmul_kernel(x_ref, g_ref, w_ref, o_ref, xn_ref):
    @pl.when(pl.program_id(1) == 0)
    def _():
        xn_ref[...] = _rms(x_ref[...], g_ref[...]).astype(BF16)

    o_ref[...] = jnp.dot(xn_ref[...], w_ref[...].astype(BF16), preferred_element_type=F32).astype(o_ref.dtype)


def _norm_matmul(x2, g, w_all, layer, tm, tn):
    n, d = x2.shape
    m = w_all.shape[2]
    return pl.pallas_call(
        _norm_matmul_kernel,
        grid=(n // tm, m // tn),
        in_specs=[
            pl.BlockSpec((tm, d), lambda i, j: (i, 0)),
            pl.BlockSpec((1, d), lambda i, j: (0, 0)),
            pl.BlockSpec((None, d, tn), lambda i, j: (layer, 0, j)),
        ],
        out_specs=pl.BlockSpec((tm, tn), lambda i, j: (i, j)),
        out_shape=jax.ShapeDtypeStruct((n, m), BF16),
        scratch_shapes=[pltpu.VMEM((tm, d), BF16)],
        compiler_params=_params("parallel", "arbitrary"),
        name="norm_matmul",
    )(x2, g.reshape(1, d), w_all)


def _matmul_kernel(x_ref, w_ref, o_ref):
    o_ref[...] = jnp.dot(x_ref[...], w_ref[...], preferred_element_type=F32).astype(o_ref.dtype)


def _matmul(xn2, w_all, layer, tm, tn):
    n, d = xn2.shape
    m = w_all.shape[2]
    return pl.pallas_call(
        _matmul_kernel,
        grid=(n // tm, m // tn),
        in_specs=[
            pl.BlockSpec((tm, d), lambda i, j: (i, 0)),
            pl.BlockSpec((None, d, tn), lambda i, j: (layer, 0, j)),
        ],
        out_specs=pl.BlockSpec((tm, tn), lambda i, j: (i, j)),
        out_shape=jax.ShapeDtypeStruct((n, m), BF16),
        compiler_params=_params("parallel", "parallel"),
        name="matmul",
    )(xn2, w_all)


def _proj_residual_kernel(n_in, final, *refs):
    x_ref = refs[0]
    a_refs = refs[1:1 + n_in]
    w_refs = refs[1 + n_in:1 + 2 * n_in]
    g_ref = refs[1 + 2 * n_in]
    o_ref = refs[2 + 2 * n_in]
    y = x_ref[...]
    for a_ref, w_ref in zip(a_refs, w_refs):
        y = y + jnp.dot(a_ref[...], w_ref[...], preferred_element_type=F32)
    if final:
        y = _rms(y, g_ref[...])
    o_ref[...] = y


def _proj_residual(x2, acts, w_all, layer, final_g, final, tm):
    n, d = x2.shape
    n_in = len(acts)
    width = acts[0].shape[1]
    assert all(a.shape[1] == width for a in acts) and n_in * width == w_all.shape[1]
    in_specs = [pl.BlockSpec((tm, d), lambda i: (i, 0))]
    in_specs += [pl.BlockSpec((tm, width), lambda i: (i, 0)) for _ in acts]
    in_specs += [pl.BlockSpec((None, width, d), functools.partial(lambda k, i: (layer, k, 0), k))
                 for k in range(n_in)]
    in_specs += [pl.BlockSpec((1, d), lambda i: (0, 0))]
    ws = [w_all] * n_in
    return pl.pallas_call(
        functools.partial(_proj_residual_kernel, n_in, final),
        grid=(n // tm,),
        in_specs=in_specs,
        out_specs=pl.BlockSpec((tm, d), lambda i: (i, 0)),
        out_shape=jax.ShapeDtypeStruct((n, d), F32),
        compiler_params=_params("parallel"),
        name="proj_residual",
    )(x2, *acts, *ws, final_g.reshape(1, d))


def _conv_kernel(tt, av_ref, ag_ref, gate_ref, hv_ref, hg_ref, bv_ref, bg_ref, wdw_ref, bdw_ref,
                 lng_ref, lnb_ref, wpw_ref, bpw_ref, o_ref, ubuf, shifted, ybuf):
    i = pl.program_id(1)
    bv = bv_ref[...]
    bg = bg_ref[...]

    def glu(v, g):
        return (v.astype(F32) + bv) * jax.nn.sigmoid(g.astype(F32) + bg)

    ubuf[0:CONV_HALO, :] = jnp.where(i > 0, glu(hv_ref[...], hg_ref[...]), 0.0)
    ubuf[CONV_HALO:, :] = glu(av_ref[...], ag_ref[...])
    span = tt + CONV_HALO - SUBLANES
    for b in range(1, SUBLANES):
        shifted[b - 1, 0:span, :] = ubuf[b:b + span, :]
    shift = CONV_HALO - (CONV_KERNEL - 1)
    c = ubuf.shape[1]
    for r0 in range(0, tt, CONV_ROWS):
        for c0 in range(0, c, LANES):
            acc = jnp.broadcast_to(bdw_ref[:, c0:c0 + LANES], (CONV_ROWS, LANES))
            for j in range(CONV_KERNEL):
                a, b = divmod(shift + j, SUBLANES)
                rows = slice(r0 + a * SUBLANES, r0 + a * SUBLANES + CONV_ROWS)
                src = ubuf[rows, c0:c0 + LANES] if b == 0 else shifted[b - 1, rows, c0:c0 + LANES]
                acc = acc + wdw_ref[j:j + 1, c0:c0 + LANES] * src
            ybuf[r0:r0 + CONV_ROWS, c0:c0 + LANES] = acc
    y = ybuf[...]
    mu = jnp.mean(y, axis=-1, keepdims=True)
    yc = y - mu
    var = jnp.mean(yc * yc, axis=-1, keepdims=True)
    y = _silu(yc * lax.rsqrt(var + EPS) * lng_ref[...] + lnb_ref[...])
    y = jnp.dot(y.astype(BF16), wpw_ref[...], preferred_element_type=F32) + bpw_ref[...]
    o_ref[...] = (y * _silu(gate_ref[...].astype(F32))).astype(o_ref.dtype)


def _conv_branch(z3, b_glu, w_dw, b_dw, ln_g, ln_b, w_pw_all, layer, b_pw, tt):
    b, t, _ = z3.shape
    c = w_pw_all.shape[1]
    hb = tt // CONV_HALO
    row = lambda v: v.reshape(1, c).astype(F32)
    w_dw_p = jnp.pad(w_dw.astype(F32), ((0, 32 - CONV_KERNEL), (0, 0)))
    const = lambda shape: pl.BlockSpec(shape, lambda bi, i: (0, 0))
    return pl.pallas_call(
        functools.partial(_conv_kernel, tt),
        grid=(b, t // tt),
        in_specs=[
            pl.BlockSpec((None, tt, c), lambda bi, i: (bi, i, 0)),
            pl.BlockSpec((None, tt, c), lambda bi, i: (bi, i, 1)),
            pl.BlockSpec((None, tt, c), lambda bi, i: (bi, i, 2)),
            pl.BlockSpec((None, CONV_HALO, c), lambda bi, i: (bi, jnp.maximum(i * hb - 1, 0), 0)),
            pl.BlockSpec((None, CONV_HALO, c), lambda bi, i: (bi, jnp.maximum(i * hb - 1, 0), 1)),
            const((1, c)), const((1, c)), const((32, c)), const((1, c)), const((1, c)), const((1, c)),
            pl.BlockSpec((None, c, c), lambda bi, i: (layer, 0, 0)), const((1, c)),
        ],
        out_specs=pl.BlockSpec((None, tt, c), lambda bi, i: (bi, i, 0)),
        out_shape=jax.ShapeDtypeStruct((b, t, c), BF16),
        scratch_shapes=[
            pltpu.VMEM((tt + CONV_HALO, c), F32),
            pltpu.VMEM((SUBLANES - 1, tt + CONV_HALO, c), F32),
            pltpu.VMEM((tt, c), F32),
        ],
        compiler_params=_params("parallel", "parallel"),
        name="conv_branch",
    )(z3, z3, z3, z3, z3, row(b_glu[:c]), row(b_glu[c:]), w_dw_p, row(b_dw), row(ln_g), row(ln_b),
      w_pw_all, row(b_pw))


def _flash_update(s_chunks, m_old, l_old):
    m_new = jnp.maximum(m_old, jnp.max(functools.reduce(jnp.maximum, s_chunks), axis=1, keepdims=True))
    alpha = jnp.exp2(m_old - m_new)
    p = [jnp.exp2(s - m_new) for s in s_chunks]
    return m_new, alpha, alpha * l_old + functools.reduce(jnp.add, p), p


def _moba_kernel(nb, heads, q_ref, k_ref, v_ref, g_ref, o_ref, kmean_ref, qa_ref, m_ref, l_ref, acc_ref):
    i = pl.program_id(1)
    blk, dh = MOBA_BLOCK, MOBA_HEAD_DIM
    nch = blk // LANES
    scale = dh ** -0.5 * LOG2E
    nbp = kmean_ref.shape[1]
    head = lambda h: slice(h * dh, (h + 1) * dh)

    @pl.when(i == 0)
    def _():
        kmean_ref[...] = jnp.zeros_like(kmean_ref)
        for n in range(nb):
            mean = jnp.mean(k_ref[n * blk:(n + 1) * blk, :].astype(F32), axis=0, keepdims=True)
            for h in range(heads):
                kmean_ref[h, n:n + 1, :] = mean[:, head(h)]

    n_iota = lax.broadcasted_iota(I32, (nbp, blk), 0)
    past = n_iota < i
    for h in range(heads):
        gate_t = lax.dot_general(kmean_ref[h].astype(BF16), q_ref[:, head(h)], NT_DIMS,
                                 preferred_element_type=F32)
        gate_t = jnp.where(past, gate_t, -jnp.inf)
        rank = jnp.zeros((nbp, blk), I32)
        for jp in range(nb - 1):
            row = gate_t[jp:jp + 1, :]
            rank = rank + jnp.where(row > gate_t, 1, jnp.where(row == gate_t, jnp.where(jp < n_iota, 1, 0), 0))
        pen_t = jnp.where(past, jnp.where(rank < MOBA_TOPK, 0.0, NEG), NEG)
        pen_t = jnp.concatenate([pen_t, jnp.zeros((LANES - nbp, blk), F32)], axis=0)
        qa_ref[h] = jnp.concatenate([q_ref[:, head(h)], pen_t.T.astype(BF16)], axis=1)

    m_ref[...] = jnp.full_like(m_ref, NEG)
    l_ref[...] = jnp.zeros_like(l_ref)
    acc_ref[...] = jnp.zeros_like(acc_ref)

    def step(h, j, s):
        k0 = pl.multiple_of(j * blk, blk)
        s_chunks = [s[:, c * LANES:(c + 1) * LANES] for c in range(nch)]
        m_new, alpha, l_new, p = _flash_update(s_chunks, m_ref[h], l_ref[h])
        m_ref[h] = m_new
        l_ref[h] = l_new
        pv = jnp.dot(jnp.concatenate(p, axis=1).astype(BF16), v_ref[pl.ds(k0, blk), head(h)],
                     preferred_element_type=F32)
        acc_ref[h] = alpha * acc_ref[h] + pv

    k_lane = lax.broadcasted_iota(I32, (blk, LANES), 1)

    def past_step(j):
        k0 = pl.multiple_of(j * blk, blk)
        onehot = jnp.where(k_lane == j, 1.0, 0.0).astype(BF16)
        for h in range(heads):
            ka = jnp.concatenate([k_ref[pl.ds(k0, blk), head(h)], onehot], axis=1)
            step(h, j, lax.dot_general(qa_ref[h], ka, NT_DIMS, preferred_element_type=F32) * scale)

    def past_quad(jj, _):
        for u in range(4):
            past_step(4 * jj + u)
        return 0

    lax.fori_loop(0, i // 4, past_quad, 0)

    @pl.when(i % 4 >= 2)
    def _():
        past_step(i // 4 * 4)
        past_step(i // 4 * 4 + 1)

    @pl.when(i % 2 == 1)
    def _():
        past_step(i - 1)

    r = lax.broadcasted_iota(I32, (blk, blk), 0)
    c = lax.broadcasted_iota(I32, (blk, blk), 1)
    causal = jnp.where(c <= r, 0.0, NEG)
    k0 = pl.multiple_of(i * blk, blk)
    for h in range(heads):
        s = lax.dot_general(q_ref[:, head(h)], k_ref[pl.ds(k0, blk), head(h)], NT_DIMS,
                            preferred_element_type=F32) * scale
        step(h, i, s + causal)
        l = jnp.sum(l_ref[h], axis=1, keepdims=True)
        o_ref[:, head(h)] = (acc_ref[h] / l * _silu(g_ref[:, head(h)].astype(F32))).astype(o_ref.dtype)


def _moba_branch(z3, heads, col0):
    b, t, _ = z3.shape
    nb = t // MOBA_BLOCK
    w = heads * MOBA_HEAD_DIM
    c0 = col0 // w
    return pl.pallas_call(
        functools.partial(_moba_kernel, nb, heads),
        grid=(b, nb),
        in_specs=[
            pl.BlockSpec((None, MOBA_BLOCK, w), lambda bi, i: (bi, i, c0)),
            pl.BlockSpec((None, t, w), lambda bi, i: (bi, 0, c0 + 1)),
            pl.BlockSpec((None, t, w), lambda bi, i: (bi, 0, c0 + 2)),
            pl.BlockSpec((None, MOBA_BLOCK, w), lambda bi, i: (bi, i, c0 + 3)),
        ],
        out_specs=pl.BlockSpec((None, MOBA_BLOCK, w), lambda bi, i: (bi, i, 0)),
        out_shape=jax.ShapeDtypeStruct((b, t, w), BF16),
        scratch_shapes=[
            pltpu.VMEM((heads, -(-nb // SUBLANES) * SUBLANES, MOBA_HEAD_DIM), F32),
            pltpu.VMEM((heads, MOBA_BLOCK, MOBA_HEAD_DIM + LANES), BF16),
            pltpu.VMEM((heads, MOBA_BLOCK, LANES), F32),
            pltpu.VMEM((heads, MOBA_BLOCK, LANES), F32),
            pltpu.VMEM((heads, MOBA_BLOCK, MOBA_HEAD_DIM), F32),
        ],
        compiler_params=_params("parallel", "arbitrary"),
        name="moba_branch",
    )(z3, z3, z3, z3)


def _dsa_prep_kernel(x_ref, g_ref, ws_ref, qn_ref, wqb_ref, kvn_ref, wuk_ref, wiq_ref, ikg_ref, ikb_ref,
                     qlat_ref, iq_ref, ckv_ref, ik2_ref, iw_ref, xn_ref):
    scale = DSA_HEAD_DIM ** -0.5 * LOG2E
    xn = _rms(x_ref[...], g_ref[...]).astype(BF16)
    xn_ref[...] = xn
    zs = jnp.dot(xn, ws_ref[...].astype(BF16), preferred_element_type=F32)
    o_kv = Q_LORA
    o_ik = Q_LORA + KV_LORA
    cqn = _rms(zs[:, :Q_LORA], qn_ref[...]).astype(BF16)
    q = jnp.dot(cqn, wqb_ref[...], preferred_element_type=F32).astype(BF16)
    for h in range(DSA_HEADS):
        ql = jnp.dot(q[:, h * DSA_HEAD_DIM:(h + 1) * DSA_HEAD_DIM], wuk_ref[h], preferred_element_type=F32)
        qlat_ref[h] = (ql * scale).astype(qlat_ref.dtype)
    iq = jnp.dot(cqn, wiq_ref[...], preferred_element_type=F32)
    for p in range(IDX_HEADS // 2):
        iq_ref[p] = iq[:, p * LANES:(p + 1) * LANES].astype(iq_ref.dtype)
    ckv_ref[...] = _rms(zs[:, o_kv:o_kv + KV_LORA], kvn_ref[...]).astype(ckv_ref.dtype)
    lane = lax.broadcasted_iota(I32, (1, LANES), 1)
    group = zs[:, o_ik:o_ik + LANES]
    rolled = pltpu.roll(group, IDX_DIM, axis=1)
    for half in range(2):
        valid = (lane < IDX_DIM) if half == 0 else (lane >= IDX_DIM)
        z = jnp.where(valid, group if half == 0 else rolled, 0.0)
        mu = jnp.sum(z, axis=-1, keepdims=True) * (1.0 / IDX_DIM)
        zc = jnp.where(valid, z - mu, 0.0)
        var = jnp.sum(zc * zc, axis=-1, keepdims=True) * (1.0 / IDX_DIM)
        y = zc * lax.rsqrt(var + EPS) * ikg_ref[:, half * LANES:(half + 1) * LANES] \
            + ikb_ref[:, half * LANES:(half + 1) * LANES]
        ik2_ref[:, half * LANES:(half + 1) * LANES] = y.astype(ik2_ref.dtype)
    iw_ref[...] = jnp.where(lane < IDX_HEADS, rolled, 0.0) * (IDX_HEADS ** -0.5 * IDX_DIM ** -0.5)


def _dsa_prep(x3, norm_g, w_in, q_norm, w_qb, kv_norm, w_uk, w_iq, layer, ik_g2, ik_b2, tm):
    b, t, d = x3.shape
    const = lambda shape: pl.BlockSpec(shape, lambda bi, i: (0,) * len(shape))
    stack = lambda w: pl.BlockSpec((None,) + w.shape[1:], lambda bi, i: (layer,) + (0,) * (w.ndim - 1))
    return pl.pallas_call(
        _dsa_prep_kernel,
        grid=(b, t // tm),
        in_specs=[
            pl.BlockSpec((None, tm, d), lambda bi, i: (bi, i, 0)),
            const((1, d)), pl.BlockSpec((None, d, DSA_SMALL_COLS), lambda bi, i: (layer, 0, 0)),
            const((1, Q_LORA)), stack(w_qb),
            const((1, KV_LORA)), stack(w_uk), stack(w_iq), const((1, 2 * LANES)),
            const((1, 2 * LANES)),
        ],
        out_specs=[
            pl.BlockSpec((None, DSA_HEADS, tm, KV_LORA), lambda bi, i: (bi, 0, i, 0)),
            pl.BlockSpec((None, IDX_HEADS // 2, tm, LANES), lambda bi, i: (bi, 0, i, 0)),
            pl.BlockSpec((None, tm, KV_LORA), lambda bi, i: (bi, i, 0)),
            pl.BlockSpec((None, tm, 2 * LANES), lambda bi, i: (bi, i, 0)),
            pl.BlockSpec((None, tm, LANES), lambda bi, i: (bi, i, 0)),
            pl.BlockSpec((None, tm, d), lambda bi, i: (bi, i, 0)),
        ],
        out_shape=[
            jax.ShapeDtypeStruct((b, DSA_HEADS, t, KV_LORA), BF16),
            jax.ShapeDtypeStruct((b, IDX_HEADS // 2, t, LANES), BF16),
            jax.ShapeDtypeStruct((b, t, KV_LORA), BF16),
            jax.ShapeDtypeStruct((b, t, 2 * LANES), BF16),
            jax.ShapeDtypeStruct((b, t, LANES), F32),
            jax.ShapeDtypeStruct((b, t, d), BF16),
        ],
        compiler_params=_params("parallel", "parallel"),
        name="dsa_prep",
    )(x3, norm_g.reshape(1, d), w_in, q_norm.reshape(1, Q_LORA), w_qb, kv_norm.reshape(1, KV_LORA),
      w_uk, w_iq, ik_g2, ik_b2)


def _dsa_attn_kernel(k_top, t_len, iq_ref, iw_ref, ik2_ref, qlat_ref, ckv_ref, gate_ref, wuv_ref, o_ref,
                     sc_ref, m_ref, l_ref, alpha_ref, p_ref, acc_ref):
    tq, tk = DSA_TQ, DSA_TK
    sub = SUBLANES
    i = pl.program_id(1)
    nkb = ((i + 1) * tq + tk - 1) // tk
    qpos8 = i * tq + lax.broadcasted_iota(I32, (sub, tq), 1)
    row8 = lax.broadcasted_iota(I32, (sub, tq), 0)

    def count(indicator):
        def body(j, cnt):
            k0 = pl.multiple_of(j * tk, tk)
            parts = [cnt, jnp.zeros_like(cnt), jnp.zeros_like(cnt), jnp.zeros_like(cnt)]
            for s0 in range(0, tk, COUNT_SLAB):
                slab = sc_ref[pl.ds(k0 + s0, COUNT_SLAB), :]
                for g in range(COUNT_SLAB // sub):
                    kb = slab[g * sub:(g + 1) * sub, :]
                    parts[g % 4] = parts[g % 4] + indicator(kb, k0 + s0 + g * sub + row8)
            return (parts[0] + parts[1]) + (parts[2] + parts[3])

        cnt = lax.fori_loop(0, nkb, body, jnp.zeros((sub, tq), I32)).astype(F32)
        return jnp.broadcast_to(jnp.sum(cnt, axis=0, keepdims=True), (sub, tq))

    iq_all = iq_ref[...].reshape(IDX_HEADS // 2 * tq, LANES)
    iw_t = iw_ref[...].T
    iw8 = [jnp.broadcast_to(iw_t[h:h + 1, :], (sub, tq)) for h in range(IDX_HEADS)]

    def score_blk(j):
        k0 = pl.multiple_of(j * tk, tk)
        se = lax.dot_general(ik2_ref[pl.ds(k0, tk), 0:LANES], iq_all, NT_DIMS, preferred_element_type=F32)
        so = lax.dot_general(ik2_ref[pl.ds(k0, tk), LANES:2 * LANES], iq_all, NT_DIMS,
                             preferred_element_type=F32)
        for g in range(tk // sub):
            rows = slice(g * sub, (g + 1) * sub)
            sc = jnp.zeros((sub, tq), F32)
            for p in range(IDX_HEADS // 2):
                cols = slice(p * tq, (p + 1) * tq)
                sc = sc + jnp.maximum(se[rows, cols], 0.0) * iw8[2 * p]
                sc = sc + jnp.maximum(so[rows, cols], 0.0) * iw8[2 * p + 1]
            sc_ref[pl.ds(k0 + g * sub, sub), :] = jnp.where(k0 + g * sub + row8 <= qpos8, sc, -jnp.inf)

    def score_pair(jj, _):
        score_blk(2 * jj)
        score_blk(2 * jj + 1)
        return 0

    lax.fori_loop(0, nkb // 2, score_pair, 0)

    @pl.when(nkb % 2 == 1)
    def _():
        score_blk(nkb - 1)

    def decode(code):
        bits = code ^ ((code >> 31) & 0x7FFFFFFF)
        return jnp.where(code < CODE_NEG_INF, -jnp.inf, pltpu.bitcast(bits, F32))

    def bit_body(b, carry):
        ans, n_ans = carry
        cand = ans + lax.shift_left(jnp.int32(1), 31 - b)
        cand_f = decode(cand)
        n = count(lambda sb, row: jnp.where(sb >= cand_f, 1, 0))
        return jnp.where(n >= k_top, cand, ans), jnp.where(n >= k_top, n, n_ans)

    scored = (nkb * tk).astype(F32)
    thr_code, n_ge = lax.fori_loop(
        0, 32, bit_body, (jnp.full((sub, tq), INT_MIN, I32), jnp.broadcast_to(scored, (sub, tq))))
    thr = decode(thr_code)

    excess = jnp.max(jnp.where(n_ge > k_top, 1.0, 0.0)) > 0.5
    idx_bits = (t_len - 1).bit_length()

    def tie_cut():
        need = k_top - count(lambda kb, row: jnp.where(kb > thr, 1, 0))

        def idx_body(b, d):
            cand = d + lax.shift_left(jnp.int32(1), idx_bits - 1 - b)
            below = count(lambda kb, row: jnp.where(kb == thr, jnp.where(row < cand, 1, 0), 0))
            return jnp.where(below < need, cand, d)

        d = lax.fori_loop(0, idx_bits, idx_body, jnp.zeros((sub, tq), I32))
        return jnp.where(n_ge > k_top, d, t_len)

    cut = lax.cond(excess, tie_cut, lambda: jnp.full((sub, tq), t_len, I32))

    tka = DSA_ATTN_TK
    nch = tka // LANES
    nka = ((i + 1) * tq + tka - 1) // tka
    m_ref[...] = jnp.full_like(m_ref, NEG)
    l_ref[...] = jnp.zeros_like(l_ref)
    acc_ref[...] = jnp.zeros_like(acc_ref)
    thr_b = jnp.broadcast_to(thr[0:1, :], (LANES, tq))
    cut_b = jnp.broadcast_to(cut[0:1, :], (LANES, tq))
    qpos_b = i * tq + lax.broadcasted_iota(I32, (LANES, tq), 1)
    row_b = lax.broadcasted_iota(I32, (LANES, tq), 0)

    def attn_blk(j):
        k0 = pl.multiple_of(j * tka, tka)
        kv = ckv_ref[pl.ds(k0, tka), :]
        bias = []
        for c in range(nch):
            kb = sc_ref[pl.ds(k0 + c * LANES, LANES), :]
            kpos = k0 + c * LANES + row_b
            tie = jnp.where(kb == thr_b, jnp.where(kpos <= cut_b, 0.0, NEG), NEG)
            bias.append(jnp.where(kpos <= qpos_b, jnp.where(kb > thr_b, 0.0, tie), NEG).T)
        for g in range(DSA_HEADS // DSA_HEAD_GROUP):
            grows = slice(g * DSA_HEAD_GROUP * tq, (g + 1) * DSA_HEAD_GROUP * tq)
            s_g = lax.dot_general(qlat_ref[g * DSA_HEAD_GROUP:(g + 1) * DSA_HEAD_GROUP].reshape(
                DSA_HEAD_GROUP * tq, KV_LORA), kv, NT_DIMS, preferred_element_type=F32)
            for hh in range(DSA_HEAD_GROUP):
                rows = slice((g * DSA_HEAD_GROUP + hh) * tq, (g * DSA_HEAD_GROUP + hh + 1) * tq)
                s_chunks = [s_g[hh * tq:(hh + 1) * tq, c * LANES:(c + 1) * LANES] + bias[c] for c in range(nch)]
                m_new, alpha, l_new, p = _flash_update(s_chunks, m_ref[rows], l_ref[rows])
                m_ref[rows] = m_new
                l_ref[rows] = l_new
                alpha_ref[rows] = alpha
                p_ref[rows] = jnp.concatenate(p, axis=1).astype(BF16)
            alpha = alpha_ref[grows]
            acc_ref[grows] = jnp.concatenate([alpha] * (KV_LORA // LANES), axis=1) * acc_ref[grows] \
                + jnp.dot(p_ref[grows], kv, preferred_element_type=F32)

    def attn_quad(jj, _):
        for u in range(4):
            attn_blk(4 * jj + u)
        return 0

    lax.fori_loop(0, nka // 4, attn_quad, 0)

    @pl.when(nka % 4 >= 2)
    def _():
        attn_blk(nka // 4 * 4)
        attn_blk(nka // 4 * 4 + 1)

    @pl.when(nka % 2 == 1)
    def _():
        attn_blk(nka - 1)

    for h in range(DSA_HEADS):
        rows = slice(h * tq, (h + 1) * tq)
        o_lat = (acc_ref[rows] / jnp.sum(l_ref[rows], axis=1, keepdims=True)).astype(BF16)
        o = jnp.dot(o_lat, wuv_ref[h], preferred_element_type=F32)
        cols = slice(h * DSA_HEAD_DIM, (h + 1) * DSA_HEAD_DIM)
        o_ref[:, cols] = (o * _silu(gate_ref[:, cols].astype(F32))).astype(o_ref.dtype)


def _dsa_attn(iq, iw, ik2, qlat, ckv, gate3, w_uv_all, layer):
    b, t, _ = ckv.shape
    k_top = min(DSA_TOPK_MAX, t // 4)
    tq = DSA_TQ
    width = DSA_HEADS * DSA_HEAD_DIM
    return pl.pallas_call(
        functools.partial(_dsa_attn_kernel, k_top, t),
        grid=(b, t // tq),
        in_specs=[
            pl.BlockSpec((None, IDX_HEADS // 2, tq, LANES), lambda bi, i: (bi, 0, i, 0)),
            pl.BlockSpec((None, tq, LANES), lambda bi, i: (bi, i, 0)),
            pl.BlockSpec((None, t, 2 * LANES), lambda bi, i: (bi, 0, 0)),
            pl.BlockSpec((None, DSA_HEADS, tq, KV_LORA), lambda bi, i: (bi, 0, i, 0)),
            pl.BlockSpec((None, t, KV_LORA), lambda bi, i: (bi, 0, 0)),
            pl.BlockSpec((None, tq, width), lambda bi, i: (bi, i, 0)),
            pl.BlockSpec((None,) + w_uv_all.shape[1:], lambda bi, i: (layer, 0, 0, 0)),
        ],
        out_specs=pl.BlockSpec((None, tq, width), lambda bi, i: (bi, i, 0)),
        out_shape=jax.ShapeDtypeStruct((b, t, width), BF16),
        scratch_shapes=[
            pltpu.VMEM((t, tq), F32),
            pltpu.VMEM((DSA_HEADS * tq, LANES), F32),
            pltpu.VMEM((DSA_HEADS * tq, LANES), F32),
            pltpu.VMEM((DSA_HEADS * tq, LANES), F32),
            pltpu.VMEM((DSA_HEADS * tq, DSA_ATTN_TK), BF16),
            pltpu.VMEM((DSA_HEADS * tq, KV_LORA), F32),
        ],
        compiler_params=_params("parallel", "parallel"),
        name="dsa_attn",
    )(iq, iw, ik2, qlat, ckv, gate3, w_uv_all)


def _even_layer(x2, b, t, i, norm_g, w_in_all, b_glu, w_dw, b_dw, ln_g, ln_b, w_pw_all, b_pw, w_out_all,
                final_g, final):
    c = w_pw_all.shape[1]
    heads = (w_in_all.shape[2] - 3 * c) // (4 * MOBA_HEAD_DIM)
    z = _norm_matmul(x2, norm_g, w_in_all, i, tm=IN_PROJ_ROWS, tn=IN_PROJ_COLS)
    z3 = z.reshape(b, t, z.shape[1])
    ya = _conv_branch(z3, b_glu, w_dw, b_dw, ln_g, ln_b, w_pw_all, i, b_pw, tt=CONV_TILE_ROWS)
    yb = _moba_branch(z3, heads, 3 * c)
    return _proj_residual(x2, [ya.reshape(b * t, c), yb.reshape(b * t, -1)], w_out_all, i, final_g, final,
                          tm=OUT_PROJ_ROWS)


def _odd_weights(w_in, ik_g, ik_b):
    n_odd = w_in.shape[0]
    o_gate = Q_LORA + KV_LORA + IDX_DIM + IDX_HEADS
    pad = jnp.zeros((n_odd, IDX_DIM), F32)
    ik_g2 = jnp.concatenate([ik_g, pad, pad, ik_g], axis=1)
    ik_b2 = jnp.concatenate([ik_b, pad, pad, ik_b], axis=1)
    return w_in[:, :, o_gate:].astype(BF16), ik_g2, ik_b2


def _odd_layer(x2, b, t, i, norm_g, w_in_all, w_gate_all, q_norm, w_qb_all, kv_norm, w_uk_all, w_uv_all,
               w_iq_all, ik_g2, ik_b2, w_out_all, final_g, final):
    d = x2.shape[1]
    x3 = x2.reshape(b, t, d)
    qlat, iq, ckv, ik2, iw, xn = _dsa_prep(x3, norm_g, w_in_all, q_norm, w_qb_all, kv_norm, w_uk_all, w_iq_all,
                                           i, ik_g2.reshape(1, 2 * LANES), ik_b2.reshape(1, 2 * LANES),
                                           tm=DSA_PREP_ROWS)
    gate = _matmul(xn.reshape(b * t, d), w_gate_all, i, tm=IN_PROJ_ROWS, tn=IN_PROJ_COLS)
    og = _dsa_attn(iq, iw, ik2, qlat, ckv, gate.reshape(b, t, -1), w_uv_all, i)
    return _proj_residual(x2, [og.reshape(b * t, -1)], w_out_all, i, final_g, final, tm=OUT_PROJ_ROWS)


def kernel(x, even_norm, even_w_in, even_b_glu, even_w_dw, even_b_dw, even_conv_ln_g, even_conv_ln_b,
           even_w_pw, even_b_pw, even_w_out, odd_norm, odd_w_in, odd_q_norm, odd_w_qb, odd_kv_norm,
           odd_w_uk, odd_w_uv, odd_w_iq, odd_ik_ln_g, odd_ik_ln_b, odd_w_out, final_norm):
    b, t, d = x.shape
    depth = even_norm.shape[0] + odd_norm.shape[0]
    assert t % DSA_TK == 0 and t % MOBA_BLOCK == 0 and t % CONV_TILE_ROWS == 0 and (b * t) % IN_PROJ_ROWS == 0
    assert (Q_LORA + KV_LORA) % LANES == 0 and 2 * IDX_DIM == LANES and IDX_DIM + IDX_HEADS <= LANES
    x2 = x.reshape(b * t, d)
    even_w_pw, even_w_out = even_w_pw.astype(BF16), even_w_out.astype(BF16)
    w_gate, ik_g2, ik_b2 = _odd_weights(odd_w_in, odd_ik_ln_g, odd_ik_ln_b)
    odd_w_qb, odd_w_uk, odd_w_uv = odd_w_qb.astype(BF16), odd_w_uk.astype(BF16), odd_w_uv.astype(BF16)
    odd_w_iq, odd_w_out = odd_w_iq.astype(BF16), odd_w_out.astype(BF16)
    for layer in range(depth):
        i = layer // 2
        final = layer == depth - 1
        if layer % 2 == 0:
            x2 = _even_layer(x2, b, t, i, even_norm[i], even_w_in, even_b_glu[i], even_w_dw[i], even_b_dw[i],
                             even_conv_ln_g[i], even_conv_ln_b[i], even_w_pw, even_b_pw[i], even_w_out,
                             final_norm, final)
        else:
            x2 = _odd_layer(x2, b, t, i, odd_norm[i], odd_w_in, w_gate, odd_q_norm[i], odd_w_qb, odd_kv_norm[i],
                            odd_w_uk, odd_w_uv, odd_w_iq, ik_g2[i], ik_b2[i], odd_w_out, final_norm, final)
    return x2.reshape(b, t, d)
```

```python
import functools

import jax
import jax.numpy as jnp
from jax import lax
from jax.experimental import pallas as pl
from jax.experimental.pallas import tpu as pltpu

BF16 = jnp.bfloat16
F32 = jnp.float32
I32 = jnp.int32

EPS = 1e-6
LANES = 128
SUBLANES = 8
VMEM_LIMIT = 56 * 1024 * 1024

IN_PROJ_ROWS = 1024
IN_PROJ_COLS = 1024
OUT_PROJ_ROWS = 512
CONV_TILE_ROWS = 256
DSA_PREP_ROWS = 256

CONV_KERNEL = 31
CONV_HALO = 32
CONV_ROWS = 64
MOBA_HEAD_DIM = 128
MOBA_BLOCK = 256
MOBA_TOPK = 3
DSA_HEADS = 16
DSA_HEAD_DIM = 128
Q_LORA = 512
KV_LORA = 256
IDX_HEADS = 16
IDX_DIM = 64
DSA_TOPK_MAX = 256
DSA_SMALL_COLS = -(-(Q_LORA + KV_LORA + IDX_DIM + IDX_HEADS) // LANES) * LANES
DSA_TQ = 128
DSA_TK = 512
DSA_ATTN_TK = 512
COUNT_SLAB = 64
DSA_HEAD_GROUP = 4
NEG = -1e30
LOG2E = 1.4426950408889634
INT_MIN = -2 ** 31
CODE_NEG_INF = (0xFF800000 ^ 0x7FFFFFFF) - 2 ** 32

NT_DIMS = (((1,), (1,)), ((), ()))


def _params(*sem):
    return pltpu.CompilerParams(dimension_semantics=sem, vmem_limit_bytes=VMEM_LIMIT)


def _rms(x, g):
    return x * lax.rsqrt(jnp.mean(x * x, axis=-1, keepdims=True) + EPS) * g


def _silu(x):
    return x * jax.nn.sigmoid(x)


def _norm_matmul_kernel(x_ref, g_ref, w_ref, o_ref, xn_ref):
    @pl.when(pl.program_id(1) == 0)
    def _():
        xn_ref[...] = _rms(x_ref[...], g_ref[...]).astype(BF16)

    o_ref[...] = jnp.dot(xn_ref[...], w_ref[...].astype(BF16), preferred_element_type=F32).astype(o_ref.dtype)


def _norm_matmul(x2, g, w_all, layer, tm, tn):
    n, d = x2.shape
    m = w_all.shape[2]
    return pl.pallas_call(
        _norm_matmul_kernel,
        grid=(n // tm, m // tn),
        in_specs=[
            pl.BlockSpec((tm, d), lambda i, j: (i, 0)),
            pl.BlockSpec((1, d), lambda i, j: (0, 0)),
            pl.BlockSpec((None, d, tn), lambda i, j: (layer, 0, j)),
        ],
        out_specs=pl.BlockSpec((tm, tn), lambda i, j: (i, j)),
        out_shape=jax.ShapeDtypeStruct((n, m), BF16),
        scratch_shapes=[pltpu.VMEM((tm, d), BF16)],
        compiler_params=_params("parallel", "arbitrary"),
        name="norm_matmul",
    )(x2, g.reshape(1, d), w_all)


def _matmul_kernel(x_ref, w_ref, o_ref):
    o_ref[...] = jnp.dot(x_ref[...], w_ref[...], preferred_element_type=F32).astype(o_ref.dtype)


def _matmul(xn2, w_all, layer, tm, tn):
    n, d = xn2.shape
    m = w_all.shape[2]
    return pl.pallas_call(
        _matmul_kernel,
        grid=(n // tm, m // tn),
        in_specs=[
            pl.BlockSpec((tm, d), lambda i, j: (i, 0)),
            pl.BlockSpec((None, d, tn), lambda i, j: (layer, 0, j)),
        ],
        out_specs=pl.BlockSpec((tm, tn), lambda i, j: (i, j)),
        out_shape=jax.ShapeDtypeStruct((n, m), BF16),
        compiler_params=_params("parallel", "parallel"),
        name="matmul",
    )(xn2, w_all)


def _proj_residual_kernel(n_in, final, *refs):
    x_ref = refs[0]
    a_refs = refs[1:1 + n_in]
    w_refs = refs[1 + n_in:1 + 2 * n_in]
    g_ref = refs[1 + 2 * n_in]
    o_ref = refs[2 + 2 * n_in]
    y = x_ref[...]
    for a_ref, w_ref in zip(a_refs, w_refs):
        y = y + jnp.dot(a_ref[...], w_ref[...], preferred_element_type=F32)
    if final:
        y = _rms(y, g_ref[...])
    o_ref[...] = y


def _proj_residual(x2, acts, w_all, layer, final_g, final, tm):
    n, d = x2.shape
    n_in = len(acts)
    width = acts[0].shape[1]
    assert all(a.shape[1] == width for a in acts) and n_in * width == w_all.shape[1]
    in_specs = [pl.BlockSpec((tm, d), lambda i: (i, 0))]
    in_specs += [pl.BlockSpec((tm, width), lambda i: (i, 0)) for _ in acts]
    in_specs += [pl.BlockSpec((None, width, d), functools.partial(lambda k, i: (layer, k, 0), k))
                 for k in range(n_in)]
    in_specs += [pl.BlockSpec((1, d), lambda i: (0, 0))]
    ws = [w_all] * n_in
    return pl.pallas_call(
        functools.partial(_proj_residual_kernel, n_in, final),
        grid=(n // tm,),
        in_specs=in_specs,
        out_specs=pl.BlockSpec((tm, d), lambda i: (i, 0)),
        out_shape=jax.ShapeDtypeStruct((n, d), F32),
        compiler_params=_params("parallel"),
        name="proj_residual",
    )(x2, *acts, *ws, final_g.reshape(1, d))


def _conv_kernel(tt, av_ref, ag_ref, gate_ref, hv_ref, hg_ref, bv_ref, bg_ref, wdw_ref, bdw_ref,
                 lng_ref, lnb_ref, wpw_ref, bpw_ref, o_ref, ubuf, shifted, ybuf):
    i = pl.program_id(1)
    bv = bv_ref[...]
    bg = bg_ref[...]

    def glu(v, g):
        return (v.astype(F32) + bv) * jax.nn.sigmoid(g.astype(F32) + bg)

    ubuf[0:CONV_HALO, :] = jnp.where(i > 0, glu(hv_ref[...], hg_ref[...]), 0.0)
    ubuf[CONV_HALO:, :] = glu(av_ref[...], ag_ref[...])
    span = tt + CONV_HALO - SUBLANES
    for b in range(1, SUBLANES):
        shifted[b - 1, 0:span, :] = ubuf[b:b + span, :]
    shift = CONV_HALO - (CONV_KERNEL - 1)
    c = ubuf.shape[1]
    for r0 in range(0, tt, CONV_ROWS):
        for c0 in range(0, c, LANES):
            acc = jnp.broadcast_to(bdw_ref[:, c0:c0 + LANES], (CONV_ROWS, LANES))
            for j in range(CONV_KERNEL):
                a, b = divmod(shift + j, SUBLANES)
                rows = slice(r0 + a * SUBLANES, r0 + a * SUBLANES + CONV_ROWS)
                src = ubuf[rows, c0:c0 + LANES] if b == 0 else shifted[b - 1, rows, c0:c0 + LANES]
                acc = acc + wdw_ref[j:j + 1, c0:c0 + LANES] * src
            ybuf[r0:r0 + CONV_ROWS, c0:c0 + LANES] = acc
    y = ybuf[...]
    mu = jnp.mean(y, axis=-1, keepdims=True)
    yc = y - mu
    var = jnp.mean(yc * yc, axis=-1, keepdims=True)
    y = _silu(yc * lax.rsqrt(var + EPS) * lng_ref[...] + lnb_ref[...])
    y = jnp.dot(y.astype(BF16), wpw_ref[...], preferred_element_type=F32) + bpw_ref[...]
    o_ref[...] = (y * _silu(gate_ref[...].astype(F32))).astype(o_ref.dtype)


def _conv_branch(z3, b_glu, w_dw, b_dw, ln_g, ln_b, w_pw_all, layer, b_pw, tt):
    b, t, _ = z3.shape
    c = w_pw_all.shape[1]
    hb = tt // CONV_HALO
    row = lambda v: v.reshape(1, c).astype(F32)
    w_dw_p = jnp.pad(w_dw.astype(F32), ((0, 32 - CONV_KERNEL), (0, 0)))
    const = lambda shape: pl.BlockSpec(shape, lambda bi, i: (0, 0))
    return pl.pallas_call(
        functools.partial(_conv_kernel, tt),
        grid=(b, t // tt),
        in_specs=[
            pl.BlockSpec((None, tt, c), lambda bi, i: (bi, i, 0)),
            pl.BlockSpec((None, tt, c), lambda bi, i: (bi, i, 1)),
            pl.BlockSpec((None, tt, c), lambda bi, i: (bi, i, 2)),
            pl.BlockSpec((None, CONV_HALO, c), lambda bi, i: (bi, jnp.maximum(i * hb - 1, 0), 0)),
            pl.BlockSpec((None, CONV_HALO, c), lambda bi, i: (bi, jnp.maximum(i * hb - 1, 0), 1)),
            const((1, c)), const((1, c)), const((32, c)), const((1, c)), const((1, c)), const((1, c)),
            pl.BlockSpec((None, c, c), lambda bi, i: (layer, 0, 0)), const((1, c)),
        ],
        out_specs=pl.BlockSpec((None, tt, c), lambda bi, i: (bi, i, 0)),
        out_shape=jax.ShapeDtypeStruct((b, t, c), BF16),
        scratch_shapes=[
            pltpu.VMEM((tt + CONV_HALO, c), F32),
            pltpu.VMEM((SUBLANES - 1, tt + CONV_HALO, c), F32),
            pltpu.VMEM((tt, c), F32),
        ],
        compiler_params=_params("parallel", "parallel"),
        name="conv_branch",
    )(z3, z3, z3, z3, z3, row(b_glu[:c]), row(b_glu[c:]), w_dw_p, row(b_dw), row(ln_g), row(ln_b),
      w_pw_all, row(b_pw))


def _flash_update(s_chunks, m_old, l_old):
    m_new = jnp.maximum(m_old, jnp.max(functools.reduce(jnp.maximum, s_chunks), axis=1, keepdims=True))
    alpha = jnp.exp2(m_old - m_new)
    p = [jnp.exp2(s - m_new) for s in s_chunks]
    return m_new, alpha, alpha * l_old + functools.reduce(jnp.add, p), p


def _moba_kernel(nb, heads, q_ref, k_ref, v_ref, g_ref, o_ref, kmean_ref, qa_ref, m_ref, l_ref, acc_ref):
    i = pl.program_id(1)
    blk, dh = MOBA_BLOCK, MOBA_HEAD_DIM
    nch = blk // LANES
    scale = dh ** -0.5 * LOG2E
    nbp = kmean_ref.shape[1]
    head = lambda h: slice(h * dh, (h + 1) * dh)

    @pl.when(i == 0)
    def _():
        kmean_ref[...] = jnp.zeros_like(kmean_ref)
        for n in range(nb):
            mean = jnp.mean(k_ref[n * blk:(n + 1) * blk, :].astype(F32), axis=0, keepdims=True)
            for h in range(heads):
                kmean_ref[h, n:n + 1, :] = mean[:, head(h)]

    n_iota = lax.broadcasted_iota(I32, (nbp, blk), 0)
    past = n_iota < i
    for h in range(heads):
        gate_t = lax.dot_general(kmean_ref[h].astype(BF16), q_ref[:, head(h)], NT_DIMS,
                                 preferred_element_type=F32)
        gate_t = jnp.where(past, gate_t, -jnp.inf)
        rank = jnp.zeros((nbp, blk), I32)
        for jp in range(nb - 1):
            row = gate_t[jp:jp + 1, :]
            rank = rank + jnp.where(row > gate_t, 1, jnp.where(row == gate_t, jnp.where(jp < n_iota, 1, 0), 0))
        pen_t = jnp.where(past, jnp.where(rank < MOBA_TOPK, 0.0, NEG), NEG)
        pen_t = jnp.concatenate([pen_t, jnp.zeros((LANES - nbp, blk), F32)], axis=0)
        qa_ref[h] = jnp.concatenate([q_ref[:, head(h)], pen_t.T.astype(BF16)], axis=1)

    m_ref[...] = jnp.full_like(m_ref, NEG)
    l_ref[...] = jnp.zeros_like(l_ref)
    acc_ref[...] = jnp.zeros_like(acc_ref)

    def step(h, j, s):
        k0 = pl.multiple_of(j * blk, blk)
        s_chunks = [s[:, c * LANES:(c + 1) * LANES] for c in range(nch)]
        m_new, alpha, l_new, p = _flash_update(s_chunks, m_ref[h], l_ref[h])
        m_ref[h] = m_new
        l_ref[h] = l_new
        pv = jnp.dot(jnp.concatenate(p, axis=1).astype(BF16), v_ref[pl.ds(k0, blk), head(h)],
                     preferred_element_type=F32)
        acc_ref[h] = alpha * acc_ref[h] + pv

    k_lane = lax.broadcasted_iota(I32, (blk, LANES), 1)

    def past_step(j):
        k0 = pl.multiple_of(j * blk, blk)
        onehot = jnp.where(k_lane == j, 1.0, 0.0).astype(BF16)
        for h in range(heads):
            ka = jnp.concatenate([k_ref[pl.ds(k0, blk), head(h)], onehot], axis=1)
            step(h, j, lax.dot_general(qa_ref[h], ka, NT_DIMS, preferred_element_type=F32) * scale)

    def past_quad(jj, _):
        for u in range(4):
            past_step(4 * jj + u)
        return 0

    lax.fori_loop(0, i // 4, past_quad, 0)

    @pl.when(i % 4 >= 2)
    def _():
        past_step(i // 4 * 4)
        past_step(i // 4 * 4 + 1)

    @pl.when(i % 2 == 1)
    def _():
        past_step(i - 1)

    r = lax.broadcasted_iota(I32, (blk, blk), 0)
    c = lax.broadcasted_iota(I32, (blk, blk), 1)
    causal = jnp.where(c <= r, 0.0, NEG)
    k0 = pl.multiple_of(i * blk, blk)
    for h in range(heads):
        s = lax.dot_general(q_ref[:, head(h)], k_ref[pl.ds(k0, blk), head(h)], NT_DIMS,
                            preferred_element_type=F32) * scale
        step(h, i, s + causal)
        l = jnp.sum(l_ref[h], axis=1, keepdims=True)
        o_ref[:, head(h)] = (acc_ref[h] / l * _silu(g_ref[:, head(h)].astype(F32))).astype(o_ref.dtype)


def _moba_branch(z3, heads, col0):
    b, t, _ = z3.shape
    nb = t // MOBA_BLOCK
    w = heads * MOBA_HEAD_DIM
    c0 = col0 // w
    return pl.pallas_call(
        functools.partial(_moba_kernel, nb, heads),
        grid=(b, nb),
        in_specs=[
            pl.BlockSpec((None, MOBA_BLOCK, w), lambda bi, i: (bi, i, c0)),
            pl.BlockSpec((None, t, w), lambda bi, i: (bi, 0, c0 + 1)),
            pl.BlockSpec((None, t, w), lambda bi, i: (bi, 0, c0 + 2)),
            pl.BlockSpec((None, MOBA_BLOCK, w), lambda bi, i: (bi, i, c0 + 3)),
        ],
        out_specs=pl.BlockSpec((None, MOBA_BLOCK, w), lambda bi, i: (bi, i, 0)),
        out_shape=jax.ShapeDtypeStruct((b, t, w), BF16),
        scratch_shapes=[
            pltpu.VMEM((heads, -(-nb // SUBLANES) * SUBLANES, MOBA_HEAD_DIM), F32),
            pltpu.VMEM((heads, MOBA_BLOCK, MOBA_HEAD_DIM + LANES), BF16),
            pltpu.VMEM((heads, MOBA_BLOCK, LANES), F32),
            pltpu.VMEM((heads, MOBA_BLOCK, LANES), F32),
            pltpu.VMEM((heads, MOBA_BLOCK, MOBA_HEAD_DIM), F32),
        ],
        compiler_params=_params("parallel", "arbitrary"),
        name="moba_branch",
    )(z3, z3, z3, z3)


def _dsa_prep_kernel(x_ref, g_ref, ws_ref, qn_ref, wqb_ref, kvn_ref, wuk_ref, wiq_ref, ikg_ref, ikb_ref,
                     qlat_ref, iq_ref, ckv_ref, ik2_ref, iw_ref, xn_ref):
    scale = DSA_HEAD_DIM ** -0.5 * LOG2E
    xn = _rms(x_ref[...], g_ref[...]).astype(BF16)
    xn_ref[...] = xn
    zs = jnp.dot(xn, ws_ref[...].astype(BF16), preferred_element_type=F32)
    o_kv = Q_LORA
    o_ik = Q_LORA + KV_LORA
    cqn = _rms(zs[:, :Q_LORA], qn_ref[...]).astype(BF16)
    q = jnp.dot(cqn, wqb_ref[...], preferred_element_type=F32).astype(BF16)
    for h in range(DSA_HEADS):
        ql = jnp.dot(q[:, h * DSA_HEAD_DIM:(h + 1) * DSA_HEAD_DIM], wuk_ref[h], preferred_element_type=F32)
        qlat_ref[h] = (ql * scale).astype(qlat_ref.dtype)
    iq = jnp.dot(cqn, wiq_ref[...], preferred_element_type=F32)
    for p in range(IDX_HEADS // 2):
        iq_ref[p] = iq[:, p * LANES:(p + 1) * LANES].astype(iq_ref.dtype)
    ckv_ref[...] = _rms(zs[:, o_kv:o_kv + KV_LORA], kvn_ref[...]).astype(ckv_ref.dtype)
    lane = lax.broadcasted_iota(I32, (1, LANES), 1)
    group = zs[:, o_ik:o_ik + LANES]
    rolled = pltpu.roll(group, IDX_DIM, axis=1)
    for half in range(2):
        valid = (lane < IDX_DIM) if half == 0 else (lane >= IDX_DIM)
        z = jnp.where(valid, group if half == 0 else rolled, 0.0)
        mu = jnp.sum(z, axis=-1, keepdims=True) * (1.0 / IDX_DIM)
        zc = jnp.where(valid, z - mu, 0.0)
        var = jnp.sum(zc * zc, axis=-1, keepdims=True) * (1.0 / IDX_DIM)
        y = zc * lax.rsqrt(var + EPS) * ikg_ref[:, half * LANES:(half + 1) * LANES] \
            + ikb_ref[:, half * LANES:(half + 1) * LANES]
        ik2_ref[:, half * LANES:(half + 1) * LANES] = y.astype(ik2_ref.dtype)
    iw_ref[...] = jnp.where(lane < IDX_HEADS, rolled, 0.0) * (IDX_HEADS ** -0.5 * IDX_DIM ** -0.5)


def _dsa_prep(x3, norm_g, w_in, q_norm, w_qb, kv_norm, w_uk, w_iq, layer, ik_g2, ik_b2, tm):
    b, t, d = x3.shape
    const = lambda shape: pl.BlockSpec(shape, lambda bi, i: (0,) * len(shape))
    stack = lambda w: pl.BlockSpec((None,) + w.shape[1:], lambda bi, i: (layer,) + (0,) * (w.ndim - 1))
    return pl.pallas_call(
        _dsa_prep_kernel,
        grid=(b, t // tm),
        in_specs=[
            pl.BlockSpec((None, tm, d), lambda bi, i: (bi, i, 0)),
            const((1, d)), pl.BlockSpec((None, d, DSA_SMALL_COLS), lambda bi, i: (layer, 0, 0)),
            const((1, Q_LORA)), stack(w_qb),
            const((1, KV_LORA)), stack(w_uk), stack(w_iq), const((1, 2 * LANES)),
            const((1, 2 * LANES)),
        ],
        out_specs=[
            pl.BlockSpec((None, DSA_HEADS, tm, KV_LORA), lambda bi, i: (bi, 0, i, 0)),
            pl.BlockSpec((None, IDX_HEADS // 2, tm, LANES), lambda bi, i: (bi, 0, i, 0)),
            pl.BlockSpec((None, tm, KV_LORA), lambda bi, i: (bi, i, 0)),
            pl.BlockSpec((None, tm, 2 * LANES), lambda bi, i: (bi, i, 0)),
            pl.BlockSpec((None, tm, LANES), lambda bi, i: (bi, i, 0)),
            pl.BlockSpec((None, tm, d), lambda bi, i: (bi, i, 0)),
        ],
        out_shape=[
            jax.ShapeDtypeStruct((b, DSA_HEADS, t, KV_LORA), BF16),
            jax.ShapeDtypeStruct((b, IDX_HEADS // 2, t, LANES), BF16),
            jax.ShapeDtypeStruct((b, t, KV_LORA), BF16),
            jax.ShapeDtypeStruct((b, t, 2 * LANES), BF16),
            jax.ShapeDtypeStruct((b, t, LANES), F32),
            jax.ShapeDtypeStruct((b, t, d), BF16),
        ],
        compiler_params=_params("parallel", "parallel"),
        name="dsa_prep",
    )(x3, norm_g.reshape(1, d), w_in, q_norm.reshape(1, Q_LORA), w_qb, kv_norm.reshape(1, KV_LORA),
      w_uk, w_iq, ik_g2, ik_b2)


def _dsa_attn_kernel(k_top, t_len, iq_ref, iw_ref, ik2_ref, qlat_ref, ckv_ref, gate_ref, wuv_ref, o_ref,
                     sc_ref, m_ref, l_ref, alpha_ref, p_ref, acc_ref):
    tq, tk = DSA_TQ, DSA_TK
    sub = SUBLANES
    i = pl.program_id(1)
    nkb = ((i + 1) * tq + tk - 1) // tk
    qpos8 = i * tq + lax.broadcasted_iota(I32, (sub, tq), 1)
    row8 = lax.broadcasted_iota(I32, (sub, tq), 0)

    def count(indicator):
        def body(j, cnt):
            k0 = pl.multiple_of(j * tk, tk)
            parts = [cnt, jnp.zeros_like(cnt), jnp.zeros_like(cnt), jnp.zeros_like(cnt)]
            for s0 in range(0, tk, COUNT_SLAB):
                slab = sc_ref[pl.ds(k0 + s0, COUNT_SLAB), :]
                for g in range(COUNT_SLAB // sub):
                    kb = slab[g * sub:(g + 1) * sub, :]
                    parts[g % 4] = parts[g % 4] + indicator(kb, k0 + s0 + g * sub + row8)
            return (parts[0] + parts[1]) + (parts[2] + parts[3])

        cnt = lax.fori_loop(0, nkb, body, jnp.zeros((sub, tq), I32)).astype(F32)
        return jnp.broadcast_to(jnp.sum(cnt, axis=0, keepdims=True), (sub, tq))

    iq_all = iq_ref[...].reshape(IDX_HEADS // 2 * tq, LANES)
    iw_t = iw_ref[...].T
    iw8 = [jnp.broadcast_to(iw_t[h:h + 1, :], (sub, tq)) for h in range(IDX_HEADS)]

    def score_blk(j):
        k0 = pl.multiple_of(j * tk, tk)
        se = lax.dot_general(ik2_ref[pl.ds(k0, tk), 0:LANES], iq_all, NT_DIMS, preferred_element_type=F32)
        so = lax.dot_general(ik2_ref[pl.ds(k0, tk), LANES:2 * LANES], iq_all, NT_DIMS,
                             preferred_element_type=F32)
        for g in range(tk // sub):
            rows = slice(g * sub, (g + 1) * sub)
            sc = jnp.zeros((sub, tq), F32)
            for p in range(IDX_HEADS // 2):
                cols = slice(p * tq, (p + 1) * tq)
                sc = sc + jnp.maximum(se[rows, cols], 0.0) * iw8[2 * p]
                sc = sc + jnp.maximum(so[rows, cols], 0.0) * iw8[2 * p + 1]
            sc_ref[pl.ds(k0 + g * sub, sub), :] = jnp.where(k0 + g * sub + row8 <= qpos8, sc, -jnp.inf)

    def score_pair(jj, _):
        score_blk(2 * jj)
        score_blk(2 * jj + 1)
        return 0

    lax.fori_loop(0, nkb // 2, score_pair, 0)

    @pl.when(nkb % 2 == 1)
    def _():
        score_blk(nkb - 1)

    def decode(code):
        bits = code ^ ((code >> 31) & 0x7FFFFFFF)
        return jnp.where(code < CODE_NEG_INF, -jnp.inf, pltpu.bitcast(bits, F32))

    def bit_body(b, carry):
        ans, n_ans = carry
        cand = ans + lax.shift_left(jnp.int32(1), 31 - b)
        cand_f = decode(cand)
        n = count(lambda sb, row: jnp.where(sb >= cand_f, 1, 0))
        return jnp.where(n >= k_top, cand, ans), jnp.where(n >= k_top, n, n_ans)

    scored = (nkb * tk).astype(F32)
    thr_code, n_ge = lax.fori_loop(
        0, 32, bit_body, (jnp.full((sub, tq), INT_MIN, I32), jnp.broadcast_to(scored, (sub, tq))))
    thr = decode(thr_code)

    excess = jnp.max(jnp.where(n_ge > k_top, 1.0, 0.0)) > 0.5
    idx_bits = (t_len - 1).bit_length()

    def tie_cut():
        need = k_top - count(lambda kb, row: jnp.where(kb > thr, 1, 0))

        def idx_body(b, d):
            cand = d + lax.shift_left(jnp.int32(1), idx_bits - 1 - b)
            below = count(lambda kb, row: jnp.where(kb == thr, jnp.where(row < cand, 1, 0), 0))
            return jnp.where(below < need, cand, d)

        d = lax.fori_loop(0, idx_bits, idx_body, jnp.zeros((sub, tq), I32))
        return jnp.where(n_ge > k_top, d, t_len)

    cut = lax.cond(excess, tie_cut, lambda: jnp.full((sub, tq), t_len, I32))

    tka = DSA_ATTN_TK
    nch = tka // LANES
    nka = ((i + 1) * tq + tka - 1) // tka
    m_ref[...] = jnp.full_like(m_ref, NEG)
    l_ref[...] = jnp.zeros_like(l_ref)
    acc_ref[...] = jnp.zeros_like(acc_ref)
    thr_b = jnp.broadcast_to(thr[0:1, :], (LANES, tq))
    cut_b = jnp.broadcast_to(cut[0:1, :], (LANES, tq))
    qpos_b = i * tq + lax.broadcasted_iota(I32, (LANES, tq), 1)
    row_b = lax.broadcasted_iota(I32, (LANES, tq), 0)

    def attn_blk(j):
        k0 = pl.multiple_of(j * tka, tka)
        kv = ckv_ref[pl.ds(k0, tka), :]
        bias = []
        for c in range(nch):
            kb = sc_ref[pl.ds(k0 + c * LANES, LANES), :]
            kpos = k0 + c * LANES + row_b
            tie = jnp.where(kb == thr_b, jnp.where(kpos <= cut_b, 0.0, NEG), NEG)
            bias.append(jnp.where(kpos <= qpos_b, jnp.where(kb > thr_b, 0.0, tie), NEG).T)
        for g in range(DSA_HEADS // DSA_HEAD_GROUP):
            grows = slice(g * DSA_HEAD_GROUP * tq, (g + 1) * DSA_HEAD_GROUP * tq)
            s_g = lax.dot_general(qlat_ref[g * DSA_HEAD_GROUP:(g + 1) * DSA_HEAD_GROUP].reshape(
                DSA_HEAD_GROUP * tq, KV_LORA), kv, NT_DIMS, preferred_element_type=F32)
            for hh in range(DSA_HEAD_GROUP):
                rows = slice((g * DSA_HEAD_GROUP + hh) * tq, (g * DSA_HEAD_GROUP + hh + 1) * tq)
                s_chunks = [s_g[hh * tq:(hh + 1) * tq, c * LANES:(c + 1) * LANES] + bias[c] for c in range(nch)]
                m_new, alpha, l_new, p = _flash_update(s_chunks, m_ref[rows], l_ref[rows])
                m_ref[rows] = m_new
                l_ref[rows] = l_new
                alpha_ref[rows] = alpha
                p_ref[rows] = jnp.concatenate(p, axis=1).astype(BF16)
            alpha = alpha_ref[grows]
            acc_ref[grows] = jnp.concatenate([alpha] * (KV_LORA // LANES), axis=1) * acc_ref[grows] \
                + jnp.dot(p_ref[grows], kv, preferred_element_type=F32)

    def attn_quad(jj, _):
        for u in range(4):
            attn_blk(4 * jj + u)
        return 0

    lax.fori_loop(0, nka // 4, attn_quad, 0)

    @pl.when(nka % 4 >= 2)
    def _():
        attn_blk(nka // 4 * 4)
        attn_blk(nka // 4 * 4 + 1)

    @pl.when(nka % 2 == 1)
    def _():
        attn_blk(nka - 1)

    for h in range(DSA_HEADS):
        rows = slice(h * tq, (h + 1) * tq)
        o_lat = (acc_ref[rows] / jnp.sum(l_ref[rows], axis=1, keepdims=True)).astype(BF16)
        o = jnp.dot(o_lat, wuv_ref[h], preferred_element_type=F32)
        cols = slice(h * DSA_HEAD_DIM, (h + 1) * DSA_HEAD_DIM)
        o_ref[:, cols] = (o * _silu(gate_ref[:, cols].astype(F32))).astype(o_ref.dtype)


def _dsa_attn(iq, iw, ik2, qlat, ckv, gate3, w_uv_all, layer):
    b, t, _ = ckv.shape
    k_top = min(DSA_TOPK_MAX, t // 4)
    tq = DSA_TQ
    width = DSA_HEADS * DSA_HEAD_DIM
    return pl.pallas_call(
        functools.partial(_dsa_attn_kernel, k_top, t),
        grid=(b, t // tq),
        in_specs=[
            pl.BlockSpec((None, IDX_HEADS // 2, tq, LANES), lambda bi, i: (bi, 0, i, 0)),
            pl.BlockSpec((None, tq, LANES), lambda bi, i: (bi, i, 0)),
            pl.BlockSpec((None, t, 2 * LANES), lambda bi, i: (bi, 0, 0)),
            pl.BlockSpec((None, DSA_HEADS, tq, KV_LORA), lambda bi, i: (bi, 0, i, 0)),
            pl.BlockSpec((None, t, KV_LORA), lambda bi, i: (bi, 0, 0)),
            pl.BlockSpec((None, tq, width), lambda bi, i: (bi, i, 0)),
            pl.BlockSpec((None,) + w_uv_all.shape[1:], lambda bi, i: (layer, 0, 0, 0)),
        ],
        out_specs=pl.BlockSpec((None, tq, width), lambda bi, i: (bi, i, 0)),
        out_shape=jax.ShapeDtypeStruct((b, t, width), BF16),
        scratch_shapes=[
            pltpu.VMEM((t, tq), F32),
            pltpu.VMEM((DSA_HEADS * tq, LANES), F32),
            pltpu.VMEM((DSA_HEADS * tq, LANES), F32),
            pltpu.VMEM((DSA_HEADS * tq, LANES), F32),
            pltpu.VMEM((DSA_HEADS * tq, DSA_ATTN_TK), BF16),
            pltpu.VMEM((DSA_HEADS * tq, KV_LORA), F32),
        ],
        compiler_params=_params("parallel", "parallel"),
        name="dsa_attn",
    )(iq, iw, ik2, qlat, ckv, gate3, w_uv_all)


def _even_layer(x2, b, t, i, norm_g, w_in_all, b_glu, w_dw, b_dw, ln_g, ln_b, w_pw_all, b_pw, w_out_all,
                final_g, final):
    c = w_pw_all.shape[1]
    heads = (w_in_all.shape[2] - 3 * c) // (4 * MOBA_HEAD_DIM)
    z = _norm_matmul(x2, norm_g, w_in_all, i, tm=IN_PROJ_ROWS, tn=IN_PROJ_COLS)
    z3 = z.reshape(b, t, z.shape[1])
    ya = _conv_branch(z3, b_glu, w_dw, b_dw, ln_g, ln_b, w_pw_all, i, b_pw, tt=CONV_TILE_ROWS)
    yb = _moba_branch(z3, heads, 3 * c)
    return _proj_residual(x2, [ya.reshape(b * t, c), yb.reshape(b * t, -1)], w_out_all, i, final_g, final,
                          tm=OUT_PROJ_ROWS)


def _odd_weights(w_in, ik_g, ik_b):
    n_odd = w_in.shape[0]
    o_gate = Q_LORA + KV_LORA + IDX_DIM + IDX_HEADS
    pad = jnp.zeros((n_odd, IDX_DIM), F32)
    ik_g2 = jnp.concatenate([ik_g, pad, pad, ik_g], axis=1)
    ik_b2 = jnp.concatenate([ik_b, pad, pad, ik_b], axis=1)
    return w_in[:, :, o_gate:].astype(BF16), ik_g2, ik_b2


def _odd_layer(x2, b, t, i, norm_g, w_in_all, w_gate_all, q_norm, w_qb_all, kv_norm, w_uk_all, w_uv_all,
               w_iq_all, ik_g2, ik_b2, w_out_all, final_g, final):
    d = x2.shape[1]
    x3 = x2.reshape(b, t, d)
    qlat, iq, ckv, ik2, iw, xn = _dsa_prep(x3, norm_g, w_in_all, q_norm, w_qb_all, kv_norm, w_uk_all, w_iq_all,
                                           i, ik_g2.reshape(1, 2 * LANES), ik_b2.reshape(1, 2 * LANES),
                                           tm=DSA_PREP_ROWS)
    gate = _matmul(xn.reshape(b * t, d), w_gate_all, i, tm=IN_PROJ_ROWS, tn=IN_PROJ_COLS)
    og = _dsa_attn(iq, iw, ik2, qlat, ckv, gate.reshape(b, t, -1), w_uv_all, i)
    return _proj_residual(x2, [og.reshape(b * t, -1)], w_out_all, i, final_g, final, tm=OUT_PROJ_ROWS)


def kernel(x, even_norm, even_w_in, even_b_glu, even_w_dw, even_b_dw, even_conv_ln_g, even_conv_ln_b,
           even_w_pw, even_b_pw, even_w_out, odd_norm, odd_w_in, odd_q_norm, odd_w_qb, odd_kv_norm,
           odd_w_uk, odd_w_uv, odd_w_iq, odd_ik_ln_g, odd_ik_ln_b, odd_w_out, final_norm):
    b, t, d = x.shape
    depth = even_norm.shape[0] + odd_norm.shape[0]
    assert t % DSA_TK == 0 and t % MOBA_BLOCK == 0 and t % CONV_TILE_ROWS == 0 and (b * t) % IN_PROJ_ROWS == 0
    assert (Q_LORA + KV_LORA) % LANES == 0 and 2 * IDX_DIM == LANES and IDX_DIM + IDX_HEADS <= LANES
    x2 = x.reshape(b * t, d)
    even_w_pw, even_w_out = even_w_pw.astype(BF16), even_w_out.astype(BF16)
    w_gate, ik_g2, ik_b2 = _odd_weights(odd_w_in, odd_ik_ln_g, odd_ik_ln_b)
    w_small = odd_w_in[:, :, :DSA_SMALL_COLS]
    odd_w_qb, odd_w_uk, odd_w_uv = odd_w_qb.astype(BF16), odd_w_uk.astype(BF16), odd_w_uv.astype(BF16)
    odd_w_iq, odd_w_out = odd_w_iq.astype(BF16), odd_w_out.astype(BF16)
    for layer in range(depth):
        i = layer // 2
        final = layer == depth - 1
        if layer % 2 == 0:
            x2 = _even_layer(x2, b, t, i, even_norm[i], even_w_in, even_b_glu[i], even_w_dw[i], even_b_dw[i],
                             even_conv_ln_g[i], even_conv_ln_b[i], even_w_pw, even_b_pw[i], even_w_out,
                             final_norm, final)
        else:
            x2 = _odd_layer(x2, b, t, i, odd_norm[i], w_small, w_gate, odd_q_norm[i], odd_w_qb, odd_kv_norm[i],
                            odd_w_uk, odd_w_uv, odd_w_iq, ik_g2[i], ik_b2[i], odd_w_out, final_norm, final)
    return x2.reshape(b, t, d)
```

```python
import functools

import jax
import jax.numpy as jnp
from jax import lax
from jax.experimental import pallas as pl
from jax.experimental.pallas import tpu as pltpu

BF16 = jnp.bfloat16
F32 = jnp.float32
I32 = jnp.int32

EPS = 1e-6
LANES = 128
SUBLANES = 8
VMEM_LIMIT = 56 * 1024 * 1024

IN_PROJ_ROWS = 1024
IN_PROJ_COLS = 1024
OUT_PROJ_ROWS = 512
CONV_TILE_ROWS = 256
DSA_PREP_ROWS = 256

CONV_KERNEL = 31
CONV_HALO = 32
CONV_ROWS = 64
MOBA_HEAD_DIM = 128
MOBA_BLOCK = 256
MOBA_TOPK = 3
DSA_HEADS = 16
DSA_HEAD_DIM = 128
Q_LORA = 512
KV_LORA = 256
IDX_HEADS = 16
IDX_DIM = 64
DSA_TOPK_MAX = 256
DSA_SMALL_COLS = -(-(Q_LORA + KV_LORA + IDX_DIM + IDX_HEADS) // LANES) * LANES
DSA_TQ = 128
DSA_TK = 512
DSA_ATTN_TK = 512
COUNT_SLAB = 64
DSA_HEAD_GROUP = 4
NEG = -1e30
LOG2E = 1.4426950408889634
INT_MIN = -2 ** 31
CODE_NEG_INF = (0xFF800000 ^ 0x7FFFFFFF) - 2 ** 32

NT_DIMS = (((1,), (1,)), ((), ()))


def _params(*sem):
    return pltpu.CompilerParams(dimension_semantics=sem, vmem_limit_bytes=VMEM_LIMIT)


def _rms(x, g):
    return x * lax.rsqrt(jnp.mean(x * x, axis=-1, keepdims=True) + EPS) * g


def _silu(x):
    return x * jax.nn.sigmoid(x)


def _norm_matmul_kernel(x_ref, g_ref, w_ref, o_ref, xn_ref):
    @pl.when(pl.program_id(1) == 0)
    def _():
        xn_ref[...] = _rms(x_ref[...], g_ref[...]).astype(BF16)

    o_ref[...] = jnp.dot(xn_ref[...], w_ref[...].astype(BF16), preferred_element_type=F32).astype(o_ref.dtype)


def _norm_matmul(x2, g, w_all, layer, tm, tn):
    n, d = x2.shape
    m = w_all.shape[2]
    return pl.pallas_call(
        _norm_matmul_kernel,
        grid=(n // tm, m // tn),
        in_specs=[
            pl.BlockSpec((tm, d), lambda i, j: (i, 0)),
            pl.BlockSpec((1, d), lambda i, j: (0, 0)),
            pl.BlockSpec((None, d, tn), lambda i, j: (layer, 0, j)),
        ],
        out_specs=pl.BlockSpec((tm, tn), lambda i, j: (i, j)),
        out_shape=jax.ShapeDtypeStruct((n, m), BF16),
        scratch_shapes=[pltpu.VMEM((tm, d), BF16)],
        compiler_params=_params("parallel", "arbitrary"),
        name="norm_matmul",
    )(x2, g.reshape(1, d), w_all)


def _matmul_kernel(x_ref, w_ref, o_ref):
    o_ref[...] = jnp.dot(x_ref[...], w_ref[...], preferred_element_type=F32).astype(o_ref.dtype)


def _matmul(xn2, w_all, layer, tm, tn):
    n, d = xn2.shape
    m = w_all.shape[2]
    return pl.pallas_call(
        _matmul_kernel,
        grid=(n // tm, m // tn),
        in_specs=[
            pl.BlockSpec((tm, d), lambda i, j: (i, 0)),
            pl.BlockSpec((None, d, tn), lambda i, j: (layer, 0, j)),
        ],
        out_specs=pl.BlockSpec((tm, tn), lambda i, j: (i, j)),
        out_shape=jax.ShapeDtypeStruct((n, m), BF16),
        compiler_params=_params("parallel", "parallel"),
        name="matmul",
    )(xn2, w_all)


def _proj_residual_kernel(n_in, final, *refs):
    x_ref = refs[0]
    a_refs = refs[1:1 + n_in]
    w_refs = refs[1 + n_in:1 + 2 * n_in]
    g_ref = refs[1 + 2 * n_in]
    o_ref = refs[2 + 2 * n_in]
    y = x_ref[...]
    for a_ref, w_ref in zip(a_refs, w_refs):
        y = y + jnp.dot(a_ref[...], w_ref[...], preferred_element_type=F32)
    if final:
        y = _rms(y, g_ref[...])
    o_ref[...] = y


def _proj_residual(x2, acts, w_all, layer, final_g, final, tm):
    n, d = x2.shape
    n_in = len(acts)
    width = acts[0].shape[1]
    assert all(a.shape[1] == width for a in acts) and n_in * width == w_all.shape[1]
    in_specs = [pl.BlockSpec((tm, d), lambda i: (i, 0))]
    in_specs += [pl.BlockSpec((tm, width), lambda i: (i, 0)) for _ in acts]
    in_specs += [pl.BlockSpec((None, width, d), functools.partial(lambda k, i: (layer, k, 0), k))
                 for k in range(n_in)]
    in_specs += [pl.BlockSpec((1, d), lambda i: (0, 0))]
    ws = [w_all] * n_in
    return pl.pallas_call(
        functools.partial(_proj_residual_kernel, n_in, final),
        grid=(n // tm,),
        in_specs=in_specs,
        out_specs=pl.BlockSpec((tm, d), lambda i: (i, 0)),
        out_shape=jax.ShapeDtypeStruct((n, d), F32),
        compiler_params=_params("parallel"),
        name="proj_residual",
    )(x2, *acts, *ws, final_g.reshape(1, d))


def _conv_kernel(tt, av_ref, ag_ref, gate_ref, hv_ref, hg_ref, bv_ref, bg_ref, wdw_ref, bdw_ref,
                 lng_ref, lnb_ref, wpw_ref, bpw_ref, o_ref, ubuf, shifted, ybuf):
    i = pl.program_id(1)
    bv = bv_ref[...]
    bg = bg_ref[...]

    def glu(v, g):
        return (v.astype(F32) + bv) * jax.nn.sigmoid(g.astype(F32) + bg)

    ubuf[0:CONV_HALO, :] = jnp.where(i > 0, glu(hv_ref[...], hg_ref[...]), 0.0)
    ubuf[CONV_HALO:, :] = glu(av_ref[...], ag_ref[...])
    span = tt + CONV_HALO - SUBLANES
    for b in range(1, SUBLANES):
        shifted[b - 1, 0:span, :] = ubuf[b:b + span, :]
    shift = CONV_HALO - (CONV_KERNEL - 1)
    c = ubuf.shape[1]
    for r0 in range(0, tt, CONV_ROWS):
        for c0 in range(0, c, LANES):
            acc = jnp.broadcast_to(bdw_ref[:, c0:c0 + LANES], (CONV_ROWS, LANES))
            for j in range(CONV_KERNEL):
                a, b = divmod(shift + j, SUBLANES)
                rows = slice(r0 + a * SUBLANES, r0 + a * SUBLANES + CONV_ROWS)
                src = ubuf[rows, c0:c0 + LANES] if b == 0 else shifted[b - 1, rows, c0:c0 + LANES]
                acc = acc + wdw_ref[j:j + 1, c0:c0 + LANES] * src
            ybuf[r0:r0 + CONV_ROWS, c0:c0 + LANES] = acc
    y = ybuf[...]
    mu = jnp.mean(y, axis=-1, keepdims=True)
    yc = y - mu
    var = jnp.mean(yc * yc, axis=-1, keepdims=True)
    y = _silu(yc * lax.rsqrt(var + EPS) * lng_ref[...] + lnb_ref[...])
    y = jnp.dot(y.astype(BF16), wpw_ref[...], preferred_element_type=F32) + bpw_ref[...]
    o_ref[...] = (y * _silu(gate_ref[...].astype(F32))).astype(o_ref.dtype)


def _conv_branch(z3, b_glu, w_dw, b_dw, ln_g, ln_b, w_pw_all, layer, b_pw, tt):
    b, t, _ = z3.shape
    c = w_pw_all.shape[1]
    hb = tt // CONV_HALO
    row = lambda v: v.reshape(1, c).astype(F32)
    w_dw_p = jnp.pad(w_dw.astype(F32), ((0, 32 - CONV_KERNEL), (0, 0)))
    const = lambda shape: pl.BlockSpec(shape, lambda bi, i: (0, 0))
    return pl.pallas_call(
        functools.partial(_conv_kernel, tt),
        grid=(b, t // tt),
        in_specs=[
            pl.BlockSpec((None, tt, c), lambda bi, i: (bi, i, 0)),
            pl.BlockSpec((None, tt, c), lambda bi, i: (bi, i, 1)),
            pl.BlockSpec((None, tt, c), lambda bi, i: (bi, i, 2)),
            pl.BlockSpec((None, CONV_HALO, c), lambda bi, i: (bi, jnp.maximum(i * hb - 1, 0), 0)),
            pl.BlockSpec((None, CONV_HALO, c), lambda bi, i: (bi, jnp.maximum(i * hb - 1, 0), 1)),
            const((1, c)), const((1, c)), const((32, c)), const((1, c)), const((1, c)), const((1, c)),
            pl.BlockSpec((None, c, c), lambda bi, i: (layer, 0, 0)), const((1, c)),
        ],
        out_specs=pl.BlockSpec((None, tt, c), lambda bi, i: (bi, i, 0)),
        out_shape=jax.ShapeDtypeStruct((b, t, c), BF16),
        scratch_shapes=[
            pltpu.VMEM((tt + CONV_HALO, c), F32),
            pltpu.VMEM((SUBLANES - 1, tt + CONV_HALO, c), F32),
            pltpu.VMEM((tt, c), F32),
        ],
        compiler_params=_params("parallel", "parallel"),
        name="conv_branch",
    )(z3, z3, z3, z3, z3, row(b_glu[:c]), row(b_glu[c:]), w_dw_p, row(b_dw), row(ln_g), row(ln_b),
      w_pw_all, row(b_pw))


def _flash_update(s_chunks, m_old, l_old):
    m_new = jnp.maximum(m_old, jnp.max(functools.reduce(jnp.maximum, s_chunks), axis=1, keepdims=True))
    alpha = jnp.exp2(m_old - m_new)
    p = [jnp.exp2(s - m_new) for s in s_chunks]
    return m_new, alpha, alpha * l_old + functools.reduce(jnp.add, p), p


def _moba_kernel(nb, heads, q_ref, k_ref, v_ref, g_ref, o_ref, kmean_ref, qa_ref, m_ref, l_ref, acc_ref):
    i = pl.program_id(1)
    blk, dh = MOBA_BLOCK, MOBA_HEAD_DIM
    nch = blk // LANES
    scale = dh ** -0.5 * LOG2E
    nbp = kmean_ref.shape[1]
    head = lambda h: slice(h * dh, (h + 1) * dh)

    @pl.when(i == 0)
    def _():
        kmean_ref[...] = jnp.zeros_like(kmean_ref)
        for n in range(nb):
            mean = jnp.mean(k_ref[n * blk:(n + 1) * blk, :].astype(F32), axis=0, keepdims=True)
            for h in range(heads):
                kmean_ref[h, n:n + 1, :] = mean[:, head(h)]

    n_iota = lax.broadcasted_iota(I32, (nbp, blk), 0)
    past = n_iota < i
    for h in range(heads):
        gate_t = lax.dot_general(kmean_ref[h].astype(BF16), q_ref[:, head(h)], NT_DIMS,
                                 preferred_element_type=F32)
        gate_t = jnp.where(past, gate_t, -jnp.inf)
        rank = jnp.zeros((nbp, blk), I32)
        for jp in range(nb - 1):
            row = gate_t[jp:jp + 1, :]
            rank = rank + jnp.where(row > gate_t, 1, jnp.where(row == gate_t, jnp.where(jp < n_iota, 1, 0), 0))
        pen_t = jnp.where(past, jnp.where(rank < MOBA_TOPK, 0.0, NEG), NEG)
        pen_t = jnp.concatenate([pen_t, jnp.zeros((LANES - nbp, blk), F32)], axis=0)
        qa_ref[h] = jnp.concatenate([q_ref[:, head(h)], pen_t.T.astype(BF16)], axis=1)

    m_ref[...] = jnp.full_like(m_ref, NEG)
    l_ref[...] = jnp.zeros_like(l_ref)
    acc_ref[...] = jnp.zeros_like(acc_ref)

    def step(h, j, s):
        k0 = pl.multiple_of(j * blk, blk)
        s_chunks = [s[:, c * LANES:(c + 1) * LANES] for c in range(nch)]
        m_new, alpha, l_new, p = _flash_update(s_chunks, m_ref[h], l_ref[h])
        m_ref[h] = m_new
        l_ref[h] = l_new
        pv = jnp.dot(jnp.concatenate(p, axis=1).astype(BF16), v_ref[pl.ds(k0, blk), head(h)],
                     preferred_element_type=F32)
        acc_ref[h] = alpha * acc_ref[h] + pv

    k_lane = lax.broadcasted_iota(I32, (blk, LANES), 1)

    def past_step(j):
        k0 = pl.multiple_of(j * blk, blk)
        onehot = jnp.where(k_lane == j, 1.0, 0.0).astype(BF16)
        for h in range(heads):
            ka = jnp.concatenate([k_ref[pl.ds(k0, blk), head(h)], onehot], axis=1)
            step(h, j, lax.dot_general(qa_ref[h], ka, NT_DIMS, preferred_element_type=F32) * scale)

    def past_quad(jj, _):
        for u in range(8):
            past_step(8 * jj + u)
        return 0

    lax.fori_loop(0, i // 8, past_quad, 0)

    @pl.when(i % 8 >= 4)
    def _():
        for u in range(4):
            past_step(i // 8 * 8 + u)

    @pl.when(i % 4 >= 2)
    def _():
        past_step(i // 4 * 4)
        past_step(i // 4 * 4 + 1)

    @pl.when(i % 2 == 1)
    def _():
        past_step(i - 1)

    r = lax.broadcasted_iota(I32, (blk, blk), 0)
    c = lax.broadcasted_iota(I32, (blk, blk), 1)
    causal = jnp.where(c <= r, 0.0, NEG)
    k0 = pl.multiple_of(i * blk, blk)
    for h in range(heads):
        s = lax.dot_general(q_ref[:, head(h)], k_ref[pl.ds(k0, blk), head(h)], NT_DIMS,
                            preferred_element_type=F32) * scale
        step(h, i, s + causal)
        l = jnp.sum(l_ref[h], axis=1, keepdims=True)
        o_ref[:, head(h)] = (acc_ref[h] / l * _silu(g_ref[:, head(h)].astype(F32))).astype(o_ref.dtype)


def _moba_branch(z3, heads, col0):
    b, t, _ = z3.shape
    nb = t // MOBA_BLOCK
    w = heads * MOBA_HEAD_DIM
    c0 = col0 // w
    return pl.pallas_call(
        functools.partial(_moba_kernel, nb, heads),
        grid=(b, nb),
        in_specs=[
            pl.BlockSpec((None, MOBA_BLOCK, w), lambda bi, i: (bi, i, c0)),
            pl.BlockSpec((None, t, w), lambda bi, i: (bi, 0, c0 + 1)),
            pl.BlockSpec((None, t, w), lambda bi, i: (bi, 0, c0 + 2)),
            pl.BlockSpec((None, MOBA_BLOCK, w), lambda bi, i: (bi, i, c0 + 3)),
        ],
        out_specs=pl.BlockSpec((None, MOBA_BLOCK, w), lambda bi, i: (bi, i, 0)),
        out_shape=jax.ShapeDtypeStruct((b, t, w), BF16),
        scratch_shapes=[
            pltpu.VMEM((heads, -(-nb // SUBLANES) * SUBLANES, MOBA_HEAD_DIM), F32),
            pltpu.VMEM((heads, MOBA_BLOCK, MOBA_HEAD_DIM + LANES), BF16),
            pltpu.VMEM((heads, MOBA_BLOCK, LANES), F32),
            pltpu.VMEM((heads, MOBA_BLOCK, LANES), F32),
            pltpu.VMEM((heads, MOBA_BLOCK, MOBA_HEAD_DIM), F32),
        ],
        compiler_params=_params("parallel", "arbitrary"),
        name="moba_branch",
    )(z3, z3, z3, z3)


def _dsa_prep_kernel(x_ref, g_ref, ws_ref, qn_ref, wqb_ref, kvn_ref, wuk_ref, wiq_ref, ikg_ref, ikb_ref,
                     qlat_ref, iq_ref, ckv_ref, ik2_ref, iw_ref, xn_ref):
    scale = DSA_HEAD_DIM ** -0.5 * LOG2E
    xn = _rms(x_ref[...], g_ref[...]).astype(BF16)
    xn_ref[...] = xn
    zs = jnp.dot(xn, ws_ref[...].astype(BF16), preferred_element_type=F32)
    o_kv = Q_LORA
    o_ik = Q_LORA + KV_LORA
    cqn = _rms(zs[:, :Q_LORA], qn_ref[...]).astype(BF16)
    q = jnp.dot(cqn, wqb_ref[...], preferred_element_type=F32).astype(BF16)
    for h in range(DSA_HEADS):
        ql = jnp.dot(q[:, h * DSA_HEAD_DIM:(h + 1) * DSA_HEAD_DIM], wuk_ref[h], preferred_element_type=F32)
        qlat_ref[h] = (ql * scale).astype(qlat_ref.dtype)
    iq = jnp.dot(cqn, wiq_ref[...], preferred_element_type=F32)
    for p in range(IDX_HEADS // 2):
        iq_ref[p] = iq[:, p * LANES:(p + 1) * LANES].astype(iq_ref.dtype)
    ckv_ref[...] = _rms(zs[:, o_kv:o_kv + KV_LORA], kvn_ref[...]).astype(ckv_ref.dtype)
    lane = lax.broadcasted_iota(I32, (1, LANES), 1)
    group = zs[:, o_ik:o_ik + LANES]
    rolled = pltpu.roll(group, IDX_DIM, axis=1)
    for half in range(2):
        valid = (lane < IDX_DIM) if half == 0 else (lane >= IDX_DIM)
        z = jnp.where(valid, group if half == 0 else rolled, 0.0)
        mu = jnp.sum(z, axis=-1, keepdims=True) * (1.0 / IDX_DIM)
        zc = jnp.where(valid, z - mu, 0.0)
        var = jnp.sum(zc * zc, axis=-1, keepdims=True) * (1.0 / IDX_DIM)
        y = zc * lax.rsqrt(var + EPS) * ikg_ref[:, half * LANES:(half + 1) * LANES] \
            + ikb_ref[:, half * LANES:(half + 1) * LANES]
        ik2_ref[:, half * LANES:(half + 1) * LANES] = y.astype(ik2_ref.dtype)
    iw_ref[...] = jnp.where(lane < IDX_HEADS, rolled, 0.0) * (IDX_HEADS ** -0.5 * IDX_DIM ** -0.5)


def _dsa_prep(x3, norm_g, w_in, q_norm, w_qb, kv_norm, w_uk, w_iq, layer, ik_g2, ik_b2, tm):
    b, t, d = x3.shape
    const = lambda shape: pl.BlockSpec(shape, lambda bi, i: (0,) * len(shape))
    stack = lambda w: pl.BlockSpec((None,) + w.shape[1:], lambda bi, i: (layer,) + (0,) * (w.ndim - 1))
    return pl.pallas_call(
        _dsa_prep_kernel,
        grid=(b, t // tm),
        in_specs=[
            pl.BlockSpec((None, tm, d), lambda bi, i: (bi, i, 0)),
            const((1, d)), pl.BlockSpec((None, d, DSA_SMALL_COLS), lambda bi, i: (layer, 0, 0)),
            const((1, Q_LORA)), stack(w_qb),
            const((1, KV_LORA)), stack(w_uk), stack(w_iq), const((1, 2 * LANES)),
            const((1, 2 * LANES)),
        ],
        out_specs=[
            pl.BlockSpec((None, DSA_HEADS, tm, KV_LORA), lambda bi, i: (bi, 0, i, 0)),
            pl.BlockSpec((None, IDX_HEADS // 2, tm, LANES), lambda bi, i: (bi, 0, i, 0)),
            pl.BlockSpec((None, tm, KV_LORA), lambda bi, i: (bi, i, 0)),
            pl.BlockSpec((None, tm, 2 * LANES), lambda bi, i: (bi, i, 0)),
            pl.BlockSpec((None, tm, LANES), lambda bi, i: (bi, i, 0)),
            pl.BlockSpec((None, tm, d), lambda bi, i: (bi, i, 0)),
        ],
        out_shape=[
            jax.ShapeDtypeStruct((b, DSA_HEADS, t, KV_LORA), BF16),
            jax.ShapeDtypeStruct((b, IDX_HEADS // 2, t, LANES), BF16),
            jax.ShapeDtypeStruct((b, t, KV_LORA), BF16),
            jax.ShapeDtypeStruct((b, t, 2 * LANES), BF16),
            jax.ShapeDtypeStruct((b, t, LANES), F32),
            jax.ShapeDtypeStruct((b, t, d), BF16),
        ],
        compiler_params=_params("parallel", "parallel"),
        name="dsa_prep",
    )(x3, norm_g.reshape(1, d), w_in, q_norm.reshape(1, Q_LORA), w_qb, kv_norm.reshape(1, KV_LORA),
      w_uk, w_iq, ik_g2, ik_b2)


def _dsa_attn_kernel(k_top, t_len, iq_ref, iw_ref, ik2_ref, qlat_ref, ckv_ref, gate_ref, wuv_ref, o_ref,
                     sc_ref, m_ref, l_ref, alpha_ref, p_ref, acc_ref):
    tq, tk = DSA_TQ, DSA_TK
    sub = SUBLANES
    i = pl.program_id(1)
    nkb = ((i + 1) * tq + tk - 1) // tk
    qpos8 = i * tq + lax.broadcasted_iota(I32, (sub, tq), 1)
    row8 = lax.broadcasted_iota(I32, (sub, tq), 0)

    def count(indicator):
        def body(j, cnt):
            k0 = pl.multiple_of(j * tk, tk)
            parts = [cnt, jnp.zeros_like(cnt), jnp.zeros_like(cnt), jnp.zeros_like(cnt)]
            for s0 in range(0, tk, COUNT_SLAB):
                slab = sc_ref[pl.ds(k0 + s0, COUNT_SLAB), :]
                for g in range(COUNT_SLAB // sub):
                    kb = slab[g * sub:(g + 1) * sub, :]
                    parts[g % 4] = parts[g % 4] + indicator(kb, k0 + s0 + g * sub + row8)
            return (parts[0] + parts[1]) + (parts[2] + parts[3])

        cnt = lax.fori_loop(0, nkb, body, jnp.zeros((sub, tq), I32)).astype(F32)
        return jnp.broadcast_to(jnp.sum(cnt, axis=0, keepdims=True), (sub, tq))

    iq_all = iq_ref[...].reshape(IDX_HEADS // 2 * tq, LANES)
    iw_t = iw_ref[...].T
    iw8 = [jnp.broadcast_to(iw_t[h:h + 1, :], (sub, tq)) for h in range(IDX_HEADS)]

    def score_blk(j):
        k0 = pl.multiple_of(j * tk, tk)
        se = lax.dot_general(ik2_ref[pl.ds(k0, tk), 0:LANES], iq_all, NT_DIMS, preferred_element_type=F32)
        so = lax.dot_general(ik2_ref[pl.ds(k0, tk), LANES:2 * LANES], iq_all, NT_DIMS,
                             preferred_element_type=F32)
        for g in range(tk // sub):
            rows = slice(g * sub, (g + 1) * sub)
            sc = jnp.zeros((sub, tq), F32)
            for p in range(IDX_HEADS // 2):
                cols = slice(p * tq, (p + 1) * tq)
                sc = sc + jnp.maximum(se[rows, cols], 0.0) * iw8[2 * p]
                sc = sc + jnp.maximum(so[rows, cols], 0.0) * iw8[2 * p + 1]
            sc_ref[pl.ds(k0 + g * sub, sub), :] = jnp.where(k0 + g * sub + row8 <= qpos8, sc, -jnp.inf)

    def score_pair(jj, _):
        score_blk(2 * jj)
        score_blk(2 * jj + 1)
        return 0

    lax.fori_loop(0, nkb // 2, score_pair, 0)

    @pl.when(nkb % 2 == 1)
    def _():
        score_blk(nkb - 1)

    def decode(code):
        bits = code ^ ((code >> 31) & 0x7FFFFFFF)
        return jnp.where(code < CODE_NEG_INF, -jnp.inf, pltpu.bitcast(bits, F32))

    def bit_body(b, carry):
        ans, n_ans = carry
        cand = ans + lax.shift_left(jnp.int32(1), 31 - b)
        cand_f = decode(cand)
        n = count(lambda sb, row: jnp.where(sb >= cand_f, 1, 0))
        return jnp.where(n >= k_top, cand, ans), jnp.where(n >= k_top, n, n_ans)

    scored = (nkb * tk).astype(F32)
    thr_code, n_ge = lax.fori_loop(
        0, 32, bit_body, (jnp.full((sub, tq), INT_MIN, I32), jnp.broadcast_to(scored, (sub, tq))))
    thr = decode(thr_code)

    excess = jnp.max(jnp.where(n_ge > k_top, 1.0, 0.0)) > 0.5
    idx_bits = (t_len - 1).bit_length()

    def tie_cut():
        need = k_top - count(lambda kb, row: jnp.where(kb > thr, 1, 0))

        def idx_body(b, d):
            cand = d + lax.shift_left(jnp.int32(1), idx_bits - 1 - b)
            below = count(lambda kb, row: jnp.where(kb == thr, jnp.where(row < cand, 1, 0), 0))
            return jnp.where(below < need, cand, d)

        d = lax.fori_loop(0, idx_bits, idx_body, jnp.zeros((sub, tq), I32))
        return jnp.where(n_ge > k_top, d, t_len)

    cut = lax.cond(excess, tie_cut, lambda: jnp.full((sub, tq), t_len, I32))

    tka = DSA_ATTN_TK
    nch = tka // LANES
    nka = ((i + 1) * tq + tka - 1) // tka
    m_ref[...] = jnp.full_like(m_ref, NEG)
    l_ref[...] = jnp.zeros_like(l_ref)
    acc_ref[...] = jnp.zeros_like(acc_ref)
    thr_b = jnp.broadcast_to(thr[0:1, :], (LANES, tq))
    cut_b = jnp.broadcast_to(cut[0:1, :], (LANES, tq))
    qpos_b = i * tq + lax.broadcasted_iota(I32, (LANES, tq), 1)
    row_b = lax.broadcasted_iota(I32, (LANES, tq), 0)

    def attn_blk(j):
        k0 = pl.multiple_of(j * tka, tka)
        kv = ckv_ref[pl.ds(k0, tka), :]
        bias = []
        for c in range(nch):
            kb = sc_ref[pl.ds(k0 + c * LANES, LANES), :]
            kpos = k0 + c * LANES + row_b
            tie = jnp.where(kb == thr_b, jnp.where(kpos <= cut_b, 0.0, NEG), NEG)
            bias.append(jnp.where(kpos <= qpos_b, jnp.where(kb > thr_b, 0.0, tie), NEG).T)
        for g in range(DSA_HEADS // DSA_HEAD_GROUP):
            grows = slice(g * DSA_HEAD_GROUP * tq, (g + 1) * DSA_HEAD_GROUP * tq)
            s_g = lax.dot_general(qlat_ref[g * DSA_HEAD_GROUP:(g + 1) * DSA_HEAD_GROUP].reshape(
                DSA_HEAD_GROUP * tq, KV_LORA), kv, NT_DIMS, preferred_element_type=F32)
            for hh in range(DSA_HEAD_GROUP):
                rows = slice((g * DSA_HEAD_GROUP + hh) * tq, (g * DSA_HEAD_GROUP + hh + 1) * tq)
                s_chunks = [s_g[hh * tq:(hh + 1) * tq, c * LANES:(c + 1) * LANES] + bias[c] for c in range(nch)]
                m_new, alpha, l_new, p = _flash_update(s_chunks, m_ref[rows], l_ref[rows])
                m_ref[rows] = m_new
                l_ref[rows] = l_new
                alpha_ref[rows] = alpha
                p_ref[rows] = jnp.concatenate(p, axis=1).astype(BF16)
            alpha = alpha_ref[grows]
            acc_ref[grows] = jnp.concatenate([alpha] * (KV_LORA // LANES), axis=1) * acc_ref[grows] \
                + jnp.dot(p_ref[grows], kv, preferred_element_type=F32)

    def attn_quad(jj, _):
        for u in range(4):
            attn_blk(4 * jj + u)
        return 0

    lax.fori_loop(0, nka // 4, attn_quad, 0)

    @pl.when(nka % 4 >= 2)
    def _():
        attn_blk(nka // 4 * 4)
        attn_blk(nka // 4 * 4 + 1)

    @pl.when(nka % 2 == 1)
    def _():
        attn_blk(nka - 1)

    for h in range(DSA_HEADS):
        rows = slice(h * tq, (h + 1) * tq)
        o_lat = (acc_ref[rows] / jnp.sum(l_ref[rows], axis=1, keepdims=True)).astype(BF16)
        o = jnp.dot(o_lat, wuv_ref[h], preferred_element_type=F32)
        cols = slice(h * DSA_HEAD_DIM, (h + 1) * DSA_HEAD_DIM)
        o_ref[:, cols] = (o * _silu(gate_ref[:, cols].astype(F32))).astype(o_ref.dtype)


def _dsa_attn(iq, iw, ik2, qlat, ckv, gate3, w_uv_all, layer):
    b, t, _ = ckv.shape
    k_top = min(DSA_TOPK_MAX, t // 4)
    tq = DSA_TQ
    width = DSA_HEADS * DSA_HEAD_DIM
    return pl.pallas_call(
        functools.partial(_dsa_attn_kernel, k_top, t),
        grid=(b, t // tq),
        in_specs=[
            pl.BlockSpec((None, IDX_HEADS // 2, tq, LANES), lambda bi, i: (bi, 0, i, 0)),
            pl.BlockSpec((None, tq, LANES), lambda bi, i: (bi, i, 0)),
            pl.BlockSpec((None, t, 2 * LANES), lambda bi, i: (bi, 0, 0)),
            pl.BlockSpec((None, DSA_HEADS, tq, KV_LORA), lambda bi, i: (bi, 0, i, 0)),
            pl.BlockSpec((None, t, KV_LORA), lambda bi, i: (bi, 0, 0)),
            pl.BlockSpec((None, tq, width), lambda bi, i: (bi, i, 0)),
            pl.BlockSpec((None,) + w_uv_all.shape[1:], lambda bi, i: (layer, 0, 0, 0)),
        ],
        out_specs=pl.BlockSpec((None, tq, width), lambda bi, i: (bi, i, 0)),
        out_shape=jax.ShapeDtypeStruct((b, t, width), BF16),
        scratch_shapes=[
            pltpu.VMEM((t, tq), F32),
            pltpu.VMEM((DSA_HEADS * tq, LANES), F32),
            pltpu.VMEM((DSA_HEADS * tq, LANES), F32),
            pltpu.VMEM((DSA_HEADS * tq, LANES), F32),
            pltpu.VMEM((DSA_HEADS * tq, DSA_ATTN_TK), BF16),
            pltpu.VMEM((DSA_HEADS * tq, KV_LORA), F32),
        ],
        compiler_params=_params("parallel", "parallel"),
        name="dsa_attn",
    )(iq, iw, ik2, qlat, ckv, gate3, w_uv_all)


def _even_layer(x2, b, t, i, norm_g, w_in_all, b_glu, w_dw, b_dw, ln_g, ln_b, w_pw_all, b_pw, w_out_all,
                final_g, final):
    c = w_pw_all.shape[1]
    heads = (w_in_all.shape[2] - 3 * c) // (4 * MOBA_HEAD_DIM)
    z = _norm_matmul(x2, norm_g, w_in_all, i, tm=IN_PROJ_ROWS, tn=IN_PROJ_COLS)
    z3 = z.reshape(b, t, z.shape[1])
    ya = _conv_branch(z3, b_glu, w_dw, b_dw, ln_g, ln_b, w_pw_all, i, b_pw, tt=CONV_TILE_ROWS)
    yb = _moba_branch(z3, heads, 3 * c)
    return _proj_residual(x2, [ya.reshape(b * t, c), yb.reshape(b * t, -1)], w_out_all, i, final_g, final,
                          tm=OUT_PROJ_ROWS)


def _odd_weights(w_in, ik_g, ik_b):
    n_odd = w_in.shape[0]
    o_gate = Q_LORA + KV_LORA + IDX_DIM + IDX_HEADS
    pad = jnp.zeros((n_odd, IDX_DIM), F32)
    ik_g2 = jnp.concatenate([ik_g, pad, pad, ik_g], axis=1)
    ik_b2 = jnp.concatenate([ik_b, pad, pad, ik_b], axis=1)
    return w_in[:, :, o_gate:].astype(BF16), ik_g2, ik_b2


def _odd_layer(x2, b, t, i, norm_g, w_in_all, w_gate_all, q_norm, w_qb_all, kv_norm, w_uk_all, w_uv_all,
               w_iq_all, ik_g2, ik_b2, w_out_all, final_g, final):
    d = x2.shape[1]
    x3 = x2.reshape(b, t, d)
    qlat, iq, ckv, ik2, iw, xn = _dsa_prep(x3, norm_g, w_in_all, q_norm, w_qb_all, kv_norm, w_uk_all, w_iq_all,
                                           i, ik_g2.reshape(1, 2 * LANES), ik_b2.reshape(1, 2 * LANES),
                                           tm=DSA_PREP_ROWS)
    gate = _matmul(xn.reshape(b * t, d), w_gate_all, i, tm=IN_PROJ_ROWS, tn=IN_PROJ_COLS)
    og = _dsa_attn(iq, iw, ik2, qlat, ckv, gate.reshape(b, t, -1), w_uv_all, i)
    return _proj_residual(x2, [og.reshape(b * t, -1)], w_out_all, i, final_g, final, tm=OUT_PROJ_ROWS)


def kernel(x, even_norm, even_w_in, even_b_glu, even_w_dw, even_b_dw, even_conv_ln_g, even_conv_ln_b,
           even_w_pw, even_b_pw, even_w_out, odd_norm, odd_w_in, odd_q_norm, odd_w_qb, odd_kv_norm,
           odd_w_uk, odd_w_uv, odd_w_iq, odd_ik_ln_g, odd_ik_ln_b, odd_w_out, final_norm):
    b, t, d = x.shape
    depth = even_norm.shape[0] + odd_norm.shape[0]
    assert t % DSA_TK == 0 and t % MOBA_BLOCK == 0 and t % CONV_TILE_ROWS == 0 and (b * t) % IN_PROJ_ROWS == 0
    assert (Q_LORA + KV_LORA) % LANES == 0 and 2 * IDX_DIM == LANES and IDX_DIM + IDX_HEADS <= LANES
    x2 = x.reshape(b * t, d)
    even_w_pw, even_w_out = even_w_pw.astype(BF16), even_w_out.astype(BF16)
    w_gate, ik_g2, ik_b2 = _odd_weights(odd_w_in, odd_ik_ln_g, odd_ik_ln_b)
    w_small = odd_w_in[:, :, :DSA_SMALL_COLS]
    odd_w_qb, odd_w_uk, odd_w_uv = odd_w_qb.astype(BF16), odd_w_uk.astype(BF16), odd_w_uv.astype(BF16)
    odd_w_iq, odd_w_out = odd_w_iq.astype(BF16), odd_w_out.astype(BF16)
    for layer in range(depth):
        i = layer // 2
        final = layer == depth - 1
        if layer % 2 == 0:
            x2 = _even_layer(x2, b, t, i, even_norm[i], even_w_in, even_b_glu[i], even_w_dw[i], even_b_dw[i],
                             even_conv_ln_g[i], even_conv_ln_b[i], even_w_pw, even_b_pw[i], even_w_out,
                             final_norm, final)
        else:
            x2 = _odd_layer(x2, b, t, i, odd_norm[i], w_small, w_gate, odd_q_norm[i], odd_w_qb, odd_kv_norm[i],
                            odd_w_uk, odd_w_uv, odd_w_iq, ik_g2[i], ik_b2[i], odd_w_out, final_norm, final)
    return x2.reshape(b, t, d)
```
